```python
import jax, jax.numpy as jnp
from jax import lax
import numpy as np

D_MODEL = 1024
BATCH = 4
SEQ = 8192
DEPTH = 1
DEC_BATCH = 128
DEC_SEQ = 1
PAST_LEN = 16384
PAGE_SIZE = 128

MLA_HEADS = 8
QK_NOPE = 64
QK_ROPE = 32
QK_HEAD = QK_NOPE + QK_ROPE
V_HEAD = 64
Q_LORA = 256
KV_LORA = 256
ROPE_THETA = 10000.0
Q_BLOCK = 128
MLA_SCALE = QK_HEAD ** -0.5
C_CONV = 512
CONV_W = 31
MEM_TOKENS = 256
MEM_HEADS = 4
MEM_HD = 128
MEM_W = MEM_HEADS * MEM_HD
MEM_SCALE = MEM_HD ** -0.5
N_BRANCH = 3
IN_COLS = Q_LORA + KV_LORA + QK_ROPE + 2 * C_CONV + MEM_W + N_BRANCH * D_MODEL
N_EXPERTS = 64
TOP_K = 8
D_EXPERT = 256
D_SHARED = 256
ROUTED_SCALE = 2.5
MOE_BLOCK = 128
EPS = 1e-6

kernel_name = 'mla_conformer_memory_moe_decode_step'


def rms_norm(x, g):
    xf = x.astype(jnp.float32)
    y = xf * lax.rsqrt(jnp.mean(xf * xf, axis=-1, keepdims=True) + EPS)
    return (y * g.astype(jnp.float32)).astype(x.dtype)


def layer_norm(x, g, b):
    xf = x.astype(jnp.float32)
    mu = jnp.mean(xf, axis=-1, keepdims=True)
    xc = xf - mu
    y = xc * lax.rsqrt(jnp.mean(xc * xc, axis=-1, keepdims=True) + EPS)
    return (y * g.astype(jnp.float32) + b.astype(jnp.float32)).astype(x.dtype)


def apply_rope(x, pos):
    half = QK_ROPE // 2
    inv = ROPE_THETA ** (-jnp.arange(half, dtype=jnp.float32) / half)
    ang = pos.astype(jnp.float32)[:, None] * inv[None, :]
    cos = jnp.cos(ang)[:, None, :]
    sin = jnp.sin(ang)[:, None, :]
    xf = x.astype(jnp.float32)
    x1, x2 = xf[..., :half], xf[..., half:]
    return jnp.concatenate([x1 * cos - x2 * sin, x1 * sin + x2 * cos], axis=-1).astype(x.dtype)


def split_in(p):
    sizes = (Q_LORA, KV_LORA, QK_ROPE, 2 * C_CONV, MEM_W, N_BRANCH * D_MODEL)
    cuts = [sum(sizes[:i]) for i in range(1, len(sizes))]
    return jnp.split(p, cuts, axis=-1)


def glu(a):
    u, g = jnp.split(a, 2, axis=-1)
    return u * jax.nn.sigmoid(g)


def swiglu(x, wg, wu, wd):
    return (jax.nn.silu(x @ wg) * (x @ wu)) @ wd


def mla_queries(q_lat, pos, g_q_lora, w_uq, g_q_mla):
    c_q = rms_norm(q_lat, g_q_lora)
    q = c_q @ w_uq
    q = rms_norm(q.reshape(q.shape[:-1] + (MLA_HEADS, QK_HEAD)), g_q_mla)
    return jnp.concatenate([q[..., :QK_NOPE], apply_rope(q[..., QK_NOPE:], pos)], axis=-1)


def mla_keys(c_kv, k_rope, pos, w_uk, g_k_mla):
    k_nope = jnp.einsum('...lc,chd->...lhd', c_kv, w_uk)
    k_r = jnp.broadcast_to(k_rope[..., None, :], k_nope.shape[:-1] + (QK_ROPE,))
    k = rms_norm(jnp.concatenate([k_nope, k_r], axis=-1), g_k_mla)
    return jnp.concatenate([k[..., :QK_NOPE], apply_rope(k[..., QK_NOPE:], pos)], axis=-1)


def attend_latent(q, k, c_kv, qpos, kpos, w_uv):
    s = jnp.einsum('...qhd,...khd->...hqk', q, k).astype(jnp.float32) * MLA_SCALE
    s = jnp.where(kpos[None, :] <= qpos[:, None], s, -1e30)
    p = jax.nn.softmax(s, axis=-1).astype(c_kv.dtype)
    ctx = jnp.einsum('...hqk,...kc->...qhc', p, c_kv)
    return jnp.einsum('...qhc,chv->...qhv', ctx, w_uv)


def mla_prompt(q, k, c_kv, w_uv):
    b, s = q.shape[:2]
    n_qb = s // Q_BLOCK
    kpos = jnp.arange(s)
    qb = q.reshape(b, n_qb, Q_BLOCK, MLA_HEADS, QK_HEAD).swapaxes(0, 1)

    def q_block(args):
        q_i, i = args
        qpos = i * Q_BLOCK + jnp.arange(Q_BLOCK)
        return attend_latent(q_i, k, c_kv, qpos, kpos, w_uv)

    o = lax.map(q_block, (qb, jnp.arange(n_qb)))
    return o.swapaxes(0, 1).reshape(b, s, MLA_HEADS * V_HEAD)


def mla_sample(q, c_new, kr_new, cache_latent, cache_krope, layer, page_table, w_uk, g_k_mla, w_uv):
    db, ds = q.shape[:2]
    past_len = page_table.shape[1] * cache_latent.shape[2]
    qpos = past_len + jnp.arange(ds)
    kpos = jnp.arange(past_len + ds)

    def one_seq(args):
        q_b, pt, c_b, kr_b = args
        c_all = jnp.concatenate([cache_latent[layer, pt].reshape(past_len, KV_LORA), c_b], axis=0)
        kr_all = jnp.concatenate([cache_krope[layer, pt].reshape(past_len, QK_ROPE), kr_b], axis=0)
        k = mla_keys(c_all, kr_all, kpos, w_uk, g_k_mla)
        return attend_latent(q_b, k, c_all, qpos, kpos, w_uv)

    o = lax.map(one_seq, (q, page_table, c_new, kr_new))
    return o.reshape(db, ds, MLA_HEADS * V_HEAD)


def conv_branch(hist, w_dw, b_dw, g_ln, b_ln, w_conv_out):
    y = lax.conv_general_dilated(hist, w_dw[:, None, :], window_strides=(1,), padding='VALID',
                                 dimension_numbers=('NWC', 'WIO', 'NWC'), feature_group_count=C_CONV)
    y = jax.nn.silu(layer_norm(y + b_dw, g_ln, b_ln))
    return y @ w_conv_out


def mem_kv(mem, g_mem_norm, w_mem_kv, g_k_mem):
    m = rms_norm(mem, g_mem_norm) @ w_mem_kv
    k, v = jnp.split(m, 2, axis=-1)
    k = rms_norm(k.reshape(k.shape[:-1] + (MEM_HEADS, MEM_HD)), g_k_mem)
    v = v.reshape(v.shape[:-1] + (MEM_HEADS, MEM_HD))
    return k, v


def mem_attend(q_mem, k, v, g_q_mem, w_o_mem):
    n, l = q_mem.shape[:2]
    q = rms_norm(q_mem.reshape(n, l, MEM_HEADS, MEM_HD), g_q_mem)
    s = jnp.einsum('nlhd,nmhd->nhlm', q, k).astype(jnp.float32) * MEM_SCALE
    p = jax.nn.softmax(s, axis=-1).astype(v.dtype)
    o = jnp.einsum('nhlm,nmhd->nlhd', p, v).reshape(n, l, MEM_W)
    return o @ w_o_mem


def merge_branches(gates, y_mla, y_conv, y_mem, w_out):
    g = jax.nn.sigmoid(gates.astype(jnp.float32)).astype(y_mla.dtype)
    g_a, g_b, g_c = jnp.split(g, N_BRANCH, axis=-1)
    return (g_a * y_mla + g_b * y_conv + g_c * y_mem) @ w_out


def moe(xs, w_router, b_router, w_e_gate, w_e_up, w_e_down, w_s_gate, w_s_up, w_s_down):
    n_tok, d = xs.shape
    scores = jax.nn.sigmoid(jnp.dot(xs.astype(jnp.float32), w_router.astype(jnp.float32)))
    _, top_idx = lax.top_k(scores + b_router.astype(jnp.float32), TOP_K)
    top_w = jnp.take_along_axis(scores, top_idx, axis=-1)
    top_w = top_w / jnp.sum(top_w, axis=-1, keepdims=True) * ROUTED_SCALE
    n_assign = n_tok * TOP_K
    flat_e = top_idx.reshape(-1).astype(jnp.int32)
    flat_t = jnp.arange(n_assign, dtype=jnp.int32) // TOP_K
    flat_w = top_w.reshape(-1)
    order = jnp.argsort(flat_e)
    se, st, sw = flat_e[order], flat_t[order], flat_w[order]
    counts = jnp.zeros((N_EXPERTS,), jnp.int32).at[flat_e].add(1)
    start = jnp.cumsum(counts) - counts
    padded = (counts + MOE_BLOCK - 1) // MOE_BLOCK * MOE_BLOCK
    pend = jnp.cumsum(padded)
    pstart = pend - padded
    dest = pstart[se] + jnp.arange(n_assign, dtype=jnp.int32) - start[se]
    n_blocks = (n_assign + MOE_BLOCK - 1) // MOE_BLOCK + N_EXPERTS
    n_slots = n_blocks * MOE_BLOCK
    slot_tok = jnp.full((n_slots,), n_tok, jnp.int32).at[dest].set(st)
    slot_w = jnp.zeros((n_slots,), jnp.float32).at[dest].set(sw)
    block_e = jnp.minimum(jnp.searchsorted(pend, jnp.arange(n_blocks, dtype=jnp.int32) * MOE_BLOCK, side='right'),
                          N_EXPERTS - 1)
    xs_pad = jnp.concatenate([xs, jnp.zeros((1, d), xs.dtype)], axis=0)

    def run_block(args):
        tok, wt, e = args
        xb = xs_pad[tok]
        return swiglu(xb, w_e_gate[e], w_e_up[e], w_e_down[e]) * wt[:, None].astype(xb.dtype)

    out = lax.map(run_block, (slot_tok.reshape(n_blocks, MOE_BLOCK), slot_w.reshape(n_blocks, MOE_BLOCK), block_e))
    routed = jax.ops.segment_sum(out.reshape(n_slots, d), slot_tok, num_segments=n_tok + 1)[:n_tok]
    return routed + swiglu(xs, w_s_gate, w_s_up, w_s_down)


def channel_mixer(x, g_ffn_norm, w_router, b_router, w_e_gate, w_e_up, w_e_down, w_s_gate, w_s_up, w_s_down):
    shape = x.shape
    h = rms_norm(x, g_ffn_norm).reshape(-1, shape[-1])
    return x + moe(h, w_router, b_router, w_e_gate, w_e_up, w_e_down, w_s_gate, w_s_up, w_s_down).reshape(shape)


def setup_inputs(seed: int = 0) -> dict:
    key = jax.random.key(seed)
    ks = list(jax.random.split(key, 48))
    cnt = [0]

    def nxt():
        cnt[0] += 1
        return ks[cnt[0] - 1]

    def nrm(shape, scale):
        return jax.random.normal(nxt(), shape, jnp.float32) * scale

    def gain(shape):
        return 1.0 + nrm(shape, 0.05)

    n_pages = PAST_LEN // PAGE_SIZE
    n_used = DEC_BATCH * n_pages
    n_phys = n_used + (n_used + 3) // 4
    d, L = D_MODEL, DEPTH
    x_prompt = nrm((BATCH, SEQ, d), 1.0)
    x_sample = nrm((DEC_BATCH, DEC_SEQ, d), 1.0)
    mem_prompt = nrm((BATCH, MEM_TOKENS, d), 1.0)
    cache_latent = nrm((L, n_phys, PAGE_SIZE, KV_LORA), 1.0)
    cache_krope = nrm((L, n_phys, PAGE_SIZE, QK_ROPE), 1.0)
    state_conv = nrm((L, DEC_BATCH, CONV_W - 1, C_CONV), 0.5)
    cache_mem_k = nrm((L, DEC_BATCH, MEM_TOKENS, MEM_HEADS, MEM_HD), 1.0)
    cache_mem_v = nrm((L, DEC_BATCH, MEM_TOKENS, MEM_HEADS, MEM_HD), 1.0)
    page_table = jax.random.permutation(nxt(), n_phys)[:n_used].reshape(DEC_BATCH, n_pages).astype(jnp.int32)
    return {
        'x_prompt': x_prompt, 'x_sample': x_sample, 'mem_prompt': mem_prompt,
        'cache_latent': cache_latent, 'cache_krope': cache_krope, 'state_conv': state_conv,
        'cache_mem_k': cache_mem_k, 'cache_mem_v': cache_mem_v, 'page_table': page_table,
        'g_attn_norm': gain((L, d)),
        'w_in': nrm((L, d, IN_COLS), d ** -0.5),
        'g_q_lora': gain((L, Q_LORA)),
        'w_uq': nrm((L, Q_LORA, MLA_HEADS * QK_HEAD), Q_LORA ** -0.5),
        'g_q_mla': gain((L, QK_HEAD)),
        'g_kv_lora': gain((L, KV_LORA)),
        'w_uk': nrm((L, KV_LORA, MLA_HEADS, QK_NOPE), KV_LORA ** -0.5),
        'w_uv': nrm((L, KV_LORA, MLA_HEADS, V_HEAD), KV_LORA ** -0.5),
        'g_k_mla': gain((L, QK_HEAD)),
        'w_o_mla': nrm((L, MLA_HEADS * V_HEAD, d), (MLA_HEADS * V_HEAD) ** -0.5),
        'w_dw': nrm((L, CONV_W, C_CONV), CONV_W ** -0.5),
        'b_dw': nrm((L, C_CONV), 0.02),
        'g_conv_ln': gain((L, C_CONV)),
        'b_conv_ln': nrm((L, C_CONV), 0.02),
        'w_conv_out': nrm((L, C_CONV, d), C_CONV ** -0.5),
        'g_mem_norm': gain((L, d)),
        'w_mem_kv': nrm((L, d, 2 * MEM_W), d ** -0.5),
        'g_q_mem': gain((L, MEM_HD)),
        'g_k_mem': gain((L, MEM_HD)),
        'w_o_mem': nrm((L, MEM_W, d), MEM_W ** -0.5),
        'w_out': nrm((L, d, d), d ** -0.5),
        'g_ffn_norm': gain((L, d)),
        'w_router': nrm((L, d, N_EXPERTS), d ** -0.5),
        'b_router': nrm((L, N_EXPERTS), 0.01),
        'w_e_gate': nrm((L, N_EXPERTS, d, D_EXPERT), d ** -0.5),
        'w_e_up': nrm((L, N_EXPERTS, d, D_EXPERT), d ** -0.5),
        'w_e_down': nrm((L, N_EXPERTS, D_EXPERT, d), D_EXPERT ** -0.5),
        'w_s_gate': nrm((L, d, D_SHARED), d ** -0.5),
        'w_s_up': nrm((L, d, D_SHARED), d ** -0.5),
        'w_s_down': nrm((L, D_SHARED, d), D_SHARED ** -0.5),
    }


def reference(x_prompt, x_sample, mem_prompt, cache_latent, cache_krope, state_conv, cache_mem_k, cache_mem_v,
              page_table, g_attn_norm, w_in, g_q_lora, w_uq, g_q_mla, g_kv_lora, w_uk, w_uv, g_k_mla, w_o_mla,
              w_dw, b_dw, g_conv_ln, b_conv_ln, w_conv_out, g_mem_norm, w_mem_kv, g_q_mem, g_k_mem, w_o_mem,
              w_out, g_ffn_norm, w_router, b_router, w_e_gate, w_e_up, w_e_down, w_s_gate, w_s_up, w_s_down):
    b, s, _ = x_prompt.shape
    db, ds, _ = x_sample.shape
    past_len = page_table.shape[1] * cache_latent.shape[2]
    pos_p = jnp.arange(s)
    pos_s = past_len + jnp.arange(ds)
    xp, xs = x_prompt, x_sample
    lat_p, kr_p_all, conv_p_all, mk_p_all, mv_p_all = [], [], [], [], []
    lat_s, kr_s_all, conv_s_all = [], [], []
    for l in range(DEPTH):
        hp = rms_norm(xp, g_attn_norm[l])
        q_lat, kv_lat, kr_p, conv_in, qm_p, gt_p = split_in(hp @ w_in[l])
        c_p = rms_norm(kv_lat, g_kv_lora[l])
        q_p = mla_queries(q_lat, pos_p, g_q_lora[l], w_uq[l], g_q_mla[l])
        k_p = mla_keys(c_p, kr_p, pos_p, w_uk[l], g_k_mla[l])
        y_mla_p = mla_prompt(q_p, k_p, c_p, w_uv[l]) @ w_o_mla[l]
        u_p = glu(conv_in)
        hist_p = jnp.concatenate([jnp.zeros((b, CONV_W - 1, C_CONV), u_p.dtype), u_p], axis=1)
        y_conv_p = conv_branch(hist_p, w_dw[l], b_dw[l], g_conv_ln[l], b_conv_ln[l], w_conv_out[l])
        mk_p, mv_p = mem_kv(mem_prompt, g_mem_norm[l], w_mem_kv[l], g_k_mem[l])
        y_mem_p = mem_attend(qm_p, mk_p, mv_p, g_q_mem[l], w_o_mem[l])
        xp = xp + merge_branches(gt_p, y_mla_p, y_conv_p, y_mem_p, w_out[l])
        xp = channel_mixer(xp, g_ffn_norm[l], w_router[l], b_router[l], w_e_gate[l], w_e_up[l], w_e_down[l],
                           w_s_gate[l], w_s_up[l], w_s_down[l])
        lat_p.append(c_p)
        kr_p_all.append(kr_p)
        conv_p_all.append(hist_p[:, -(CONV_W - 1):])
        mk_p_all.append(mk_p)
        mv_p_all.append(mv_p)
        hs = rms_norm(xs, g_attn_norm[l])
        q_lat_s, kv_lat_s, kr_s, conv_in_s, qm_s, gt_s = split_in(hs @ w_in[l])
        c_s = rms_norm(kv_lat_s, g_kv_lora[l])
        q_s = mla_queries(q_lat_s, pos_s, g_q_lora[l], w_uq[l], g_q_mla[l])
        y_mla_s = mla_sample(q_s, c_s, kr_s, cache_latent, cache_krope, l, page_table,
                             w_uk[l], g_k_mla[l], w_uv[l]) @ w_o_mla[l]
        hist_s = jnp.concatenate([state_conv[l], glu(conv_in_s)], axis=1)
        y_conv_s = conv_branch(hist_s, w_dw[l], b_dw[l], g_conv_ln[l], b_conv_ln[l], w_conv_out[l])
        y_mem_s = mem_attend(qm_s, cache_mem_k[l], cache_mem_v[l], g_q_mem[l], w_o_mem[l])
        xs = xs + merge_branches(gt_s, y_mla_s, y_conv_s, y_mem_s, w_out[l])
        xs = channel_mixer(xs, g_ffn_norm[l], w_router[l], b_router[l], w_e_gate[l], w_e_up[l], w_e_down[l],
                           w_s_gate[l], w_s_up[l], w_s_down[l])
        lat_s.append(c_s)
        kr_s_all.append(kr_s)
        conv_s_all.append(hist_s[:, -(CONV_W - 1):])
    return (xp, xs, jnp.stack(lat_p), jnp.stack(kr_p_all), jnp.stack(conv_p_all), jnp.stack(mk_p_all),
            jnp.stack(mv_p_all), jnp.stack(lat_s), jnp.stack(kr_s_all), jnp.stack(conv_s_all))
```

```python
import functools
import math

import jax
import jax.numpy as jnp
from jax import lax
from jax.experimental import pallas as pl
from jax.experimental.pallas import tpu as pltpu

F32 = jnp.float32
BF16 = jnp.bfloat16
I32 = jnp.int32
U32 = jnp.uint32

MLA_HEADS = 8
QK_NOPE = 64
QK_ROPE = 32
QK_HEAD = QK_NOPE + QK_ROPE
V_HEAD = 64
Q_LORA = 256
KV_LORA = 256
ROPE_THETA = 10000.0
MLA_SCALE = QK_HEAD ** -0.5
C_CONV = 512
CONV_W = 31
MEM_HEADS = 4
MEM_HD = 128
MEM_W = MEM_HEADS * MEM_HD
MEM_SCALE = MEM_HD ** -0.5
N_BRANCH = 3
N_EXPERTS = 64
TOP_K = 8
D_EXPERT = 256
ROUTED_SCALE = 2.5
EPS = 1e-6
NEG_BIG = -1e30

LANES = 128
HEAD_PAD = LANES
QKV_COLS = Q_LORA + KV_LORA + QK_ROPE
CONV_TAIL = 32
MIB = 1024 * 1024


def _dot(a, b):
    return jnp.dot(a, b, preferred_element_type=F32)


def _dot_nt(a, b):
    return lax.dot_general(a, b, (((1,), (1,)), ((), ())), preferred_element_type=F32)


def _split(x):
    hi = x.astype(BF16)
    lo = (x - hi.astype(F32)).astype(BF16)
    return hi, lo


def _rms(x, g):
    return x * lax.rsqrt(jnp.mean(x * x, axis=-1, keepdims=True) + EPS) * g


def _head_norm(x, g):
    ss = jnp.sum(x * x, axis=-1, keepdims=True)
    return x * lax.rsqrt(ss * (1.0 / QK_HEAD) + EPS) * g


def _rope128(n, cos, sa, sb):
    return n * cos + pltpu.roll(n, LANES - QK_ROPE // 2, 1) * sa + pltpu.roll(n, QK_ROPE // 2, 1) * sb


def _sigmoid(x):
    return jax.nn.sigmoid(x)


def _silu(x):
    return x * jax.nn.sigmoid(x)


def _const_spec(shape):
    nd = len(shape)
    return pl.BlockSpec(shape, lambda *_: (0,) * nd)


def _params(sem, vmem_mib):
    return pltpu.CompilerParams(dimension_semantics=sem, vmem_limit_bytes=vmem_mib * MIB)


def _pad_heads(w, used):
    lead = w.shape[:-1]
    w = w.reshape(lead + (MLA_HEADS, used))
    w = jnp.pad(w, [(0, 0)] * len(lead) + [(0, 0), (0, HEAD_PAD - used)])
    return w.reshape(lead + (MLA_HEADS * HEAD_PAD,))


def _rope_tables(pos):
    half = QK_ROPE // 2
    inv = ROPE_THETA ** (-jnp.arange(half, dtype=F32) / half)
    ang = pos.astype(F32)[:, None] * inv[None, :]
    cos, sin = jnp.cos(ang), jnp.sin(ang)
    n = pos.shape[0]
    ones = jnp.ones((n, QK_NOPE), F32)
    zeros = jnp.zeros((n, QK_NOPE), F32)
    tail1 = jnp.ones((n, HEAD_PAD - QK_HEAD), F32)
    tail0 = jnp.zeros((n, HEAD_PAD - QK_HEAD), F32)
    z16 = jnp.zeros((n, half), F32)
    cosb = jnp.concatenate([ones, cos, cos, tail1], axis=1)
    sa = jnp.concatenate([zeros, -sin, z16, tail0], axis=1)
    sb = jnp.concatenate([zeros, z16, sin, tail0], axis=1)
    return cosb, sa, sb


def _mem_kv_kernel(mem_ref, gn_ref, w_ref, gk_ref, k_ref, v_ref):
    m = _rms(mem_ref[...], gn_ref[...]).astype(BF16)
    kv = _dot(m, w_ref[...])
    for h in range(MEM_HEADS):
        sl = slice(h * MEM_HD, (h + 1) * MEM_HD)
        k_ref[:, sl] = _rms(kv[:, sl], gk_ref[...])
    v_ref[...] = kv[:, MEM_W:]


def _mem_kv(mem2d, g_norm, w_kv_bf, g_k):
    n, d = mem2d.shape
    tm = 256
    return pl.pallas_call(
        _mem_kv_kernel,
        grid=(n // tm,),
        in_specs=[pl.BlockSpec((tm, d), lambda i: (i, 0)), _const_spec((1, d)), _const_spec((d, 2 * MEM_W)),
                  _const_spec((1, MEM_HD))],
        out_specs=[pl.BlockSpec((tm, MEM_W), lambda i: (i, 0)), pl.BlockSpec((tm, MEM_W), lambda i: (i, 0))],
        out_shape=[jax.ShapeDtypeStruct((n, MEM_W), F32), jax.ShapeDtypeStruct((n, MEM_W), F32)],
        compiler_params=_params(("parallel",), 32),
        name="mem_kv",
    )(mem2d, g_norm, w_kv_bf, g_k)


def _qkv_core(x, ga, wa, gq, wuq, gqm, gkv, wuk, e32, gkm):
    h = _rms(x, ga).astype(BF16)
    pa = _dot(h, wa)
    q_lat = pa[:, :Q_LORA]
    kv_lat = pa[:, Q_LORA:Q_LORA + KV_LORA]
    kr = pa[:, Q_LORA + KV_LORA:QKV_COLS]
    c_q = _rms(q_lat, gq).astype(BF16)
    q_raw = _dot(c_q, wuq)
    c = _rms(kv_lat, gkv)
    cb = c.astype(BF16)
    kr_hi, kr_lo = _split(kr)
    k_raw = _dot(cb, wuk) + _dot(kr_hi, e32) + _dot(kr_lo, e32)
    return h, q_raw, k_raw, c, cb, kr


def _qkv_kernel(x_ref, ga_ref, wa_ref, gq_ref, wuq_ref, gqm_ref, gkv_ref, wuk_ref, e32_ref, gkm_ref, wuv_ref,
                cos_ref, sa_ref, sb_ref, q_ref, k_ref, v_ref, c_ref, kr_ref):
    _, q_raw, k_raw, c, cb, kr = _qkv_core(x_ref[...], ga_ref[...], wa_ref[...], gq_ref[...], wuq_ref[...],
                                           gqm_ref[...], gkv_ref[...], wuk_ref[...], e32_ref[...], gkm_ref[...])
    c_ref[...] = c
    kr_ref[...] = kr
    v_ref[...] = _dot(cb, wuv_ref[...]).astype(BF16)
    cos, sa, sb = cos_ref[...], sa_ref[...], sb_ref[...]
    for h in range(MLA_HEADS):
        sl = slice(h * HEAD_PAD, (h + 1) * HEAD_PAD)
        q_ref[:, sl] = _rope128(_head_norm(q_raw[:, sl], gqm_ref[:, sl]), cos, sa, sb).astype(BF16)
        k_ref[:, sl] = _rope128(_head_norm(k_raw[:, sl], gkm_ref[:, sl]), cos, sa, sb).astype(BF16)


def _qkv(x2d, seq, w, tabs):
    t, d = x2d.shape
    tm = min(512, seq)
    nseq = seq // tm
    hp = MLA_HEADS * HEAD_PAD
    row = lambda i: (i, 0)
    tab = lambda i: (i % nseq, 0)
    return pl.pallas_call(
        _qkv_kernel,
        grid=(t // tm,),
        in_specs=[pl.BlockSpec((tm, d), row), _const_spec((1, d)), _const_spec((d, QKV_COLS)),
                  _const_spec((1, Q_LORA)), _const_spec((Q_LORA, hp)), _const_spec((1, hp)),
                  _const_spec((1, KV_LORA)), _const_spec((KV_LORA, hp)), _const_spec((QK_ROPE, hp)),
                  _const_spec((1, hp)), _const_spec((KV_LORA, hp)),
                  pl.BlockSpec((tm, HEAD_PAD), tab), pl.BlockSpec((tm, HEAD_PAD), tab),
                  pl.BlockSpec((tm, HEAD_PAD), tab)],
        out_specs=[pl.BlockSpec((tm, hp), row), pl.BlockSpec((tm, hp), row), pl.BlockSpec((tm, hp), row),
                   pl.BlockSpec((tm, KV_LORA), row), pl.BlockSpec((tm, QK_ROPE), row)],
        out_shape=[jax.ShapeDtypeStruct((t, hp), BF16), jax.ShapeDtypeStruct((t, hp), BF16),
                   jax.ShapeDtypeStruct((t, hp), BF16), jax.ShapeDtypeStruct((t, KV_LORA), F32),
                   jax.ShapeDtypeStruct((t, QK_ROPE), F32)],
        compiler_params=_params(("parallel",), 48),
        name="qkv_proj",
    )(x2d, w["g_attn"], w["w_a"], w["g_q_lora"], w["w_uq_p"], w["g_q_mla_p"], w["g_kv_lora"], w["w_uk_p"],
      w["e32"], w["g_k_mla_p"], w["w_uv_p"], *tabs)


def _flash_kernel(q_ref, k_ref, v_ref, o_ref, m_scr, l_scr, acc_scr):
    i = pl.program_id(1)
    j = pl.program_id(2)
    tq = q_ref.shape[0]
    tk = k_ref.shape[0]

    @pl.when(j == 0)
    def _():
        m_scr[...] = jnp.full(m_scr.shape, NEG_BIG, F32)
        l_scr[...] = jnp.zeros(l_scr.shape, F32)
        acc_scr[...] = jnp.zeros(acc_scr.shape, F32)

    def step(masked):
        if masked:
            row = lax.broadcasted_iota(I32, (tq, tk), 0)
            col = lax.broadcasted_iota(I32, (tq, tk), 1)
            keep = col <= row
        for h in range(MLA_HEADS):
            sl = slice(h * HEAD_PAD, (h + 1) * HEAD_PAD)
            s = _dot_nt(q_ref[:, sl], k_ref[:, sl]) * MLA_SCALE
            if masked:
                s = jnp.where(keep, s, NEG_BIG)
            m_prev = m_scr[h]
            m_new = jnp.maximum(m_prev, jnp.max(s, axis=-1, keepdims=True))
            alpha = jnp.exp(m_prev - m_new)
            p = jnp.exp(s - m_new)
            l_scr[h] = alpha * l_scr[h] + jnp.sum(p, axis=-1, keepdims=True)
            acc_scr[h] = alpha * acc_scr[h] + _dot(p.astype(BF16), v_ref[:, sl])
            m_scr[h] = m_new

    @pl.when(j < i)
    def _():
        step(False)

    @pl.when(j == i)
    def _():
        step(True)
        for h in range(MLA_HEADS):
            o = acc_scr[h] / l_scr[h]
            o_ref[:, h * V_HEAD:(h + 1) * V_HEAD] = o[:, :V_HEAD].astype(BF16)


def _flash(q, k, v, batch, seq):
    t, hp = q.shape
    tq = min(512, seq)
    nq = seq // tq
    qmap = lambda b, i, j: (b * nq + i, 0)
    kmap = lambda b, i, j: (b * nq + jnp.minimum(i, j), 0)
    return pl.pallas_call(
        _flash_kernel,
        grid=(batch, nq, nq),
        in_specs=[pl.BlockSpec((tq, hp), qmap), pl.BlockSpec((tq, hp), kmap), pl.BlockSpec((tq, hp), kmap)],
        out_specs=pl.BlockSpec((tq, MLA_HEADS * V_HEAD), qmap),
        out_shape=jax.ShapeDtypeStruct((t, MLA_HEADS * V_HEAD), BF16),
        scratch_shapes=[pltpu.VMEM((MLA_HEADS, tq, 1), F32), pltpu.VMEM((MLA_HEADS, tq, 1), F32),
                        pltpu.VMEM((MLA_HEADS, tq, HEAD_PAD), F32)],
        compiler_params=_params(("parallel", "parallel", "arbitrary"), 48),
        name="flash_attn",
    )(q, k, v)


def _ln_silu(y, g, b):
    mu = jnp.mean(y, axis=-1, keepdims=True)
    yc = y - mu
    n = yc * lax.rsqrt(jnp.mean(yc * yc, axis=-1, keepdims=True) + EPS)
    return _silu(n * g + b)


def _merge_out(x, gsig, y_mla, y_conv, y_mem, w_out):
    d = x.shape[-1]
    m = gsig[:, :d] * y_mla + gsig[:, d:2 * d] * y_conv + gsig[:, 2 * d:] * y_mem
    return x + _dot(m.astype(BF16), w_out)


def _mixers_kernel(x_ref, o_ref, mk_ref, mv_ref, ga_ref, wb_ref, wdw_ref, bdw_ref, gln_ref, bln_ref, wco_ref,
                   gqm_ref, wom_ref, womla_ref, wout_ref, x1_ref, conv_ref, ubuf):
    j = pl.program_id(1)
    tm = x_ref.shape[0]
    x = x_ref[...]
    h = _rms(x, ga_ref[...]).astype(BF16)
    pb = _dot(h, wb_ref[...])
    u = pb[:, :C_CONV] * _sigmoid(pb[:, C_CONV:2 * C_CONV])
    qm = pb[:, 2 * C_CONV:2 * C_CONV + MEM_W]
    gsig = _sigmoid(pb[:, 2 * C_CONV + MEM_W:])

    @pl.when(j == 0)
    def _():
        ubuf[0:CONV_TAIL, :] = jnp.zeros((CONV_TAIL, C_CONV), F32)

    ubuf[CONV_TAIL:CONV_TAIL + tm, :] = u
    y = jnp.zeros((tm, C_CONV), F32) + bdw_ref[...]
    off = CONV_TAIL - (CONV_W - 1)
    for tap in range(CONV_W):
        y = y + wdw_ref[tap:tap + 1, :] * ubuf[off + tap:off + tap + tm, :]
    tail = ubuf[tm:tm + CONV_TAIL, :]
    ubuf[0:CONV_TAIL, :] = tail

    @pl.when(j == pl.num_programs(1) - 1)
    def _():
        conv_ref[0] = tail[CONV_TAIL - (CONV_W - 1):, :]

    y_conv = _dot(_ln_silu(y, gln_ref[...], bln_ref[...]).astype(BF16), wco_ref[...])

    mk = mk_ref[0].astype(BF16)
    mv = mv_ref[0].astype(BF16)
    heads = []
    for hd in range(MEM_HEADS):
        sl = slice(hd * MEM_HD, (hd + 1) * MEM_HD)
        qh = _rms(qm[:, sl], gqm_ref[...]).astype(BF16)
        s = _dot_nt(qh, mk[:, sl]) * MEM_SCALE
        e = jnp.exp(s - jnp.max(s, axis=-1, keepdims=True))
        p = e / jnp.sum(e, axis=-1, keepdims=True)
        heads.append(_dot(p.astype(BF16), mv[:, sl]))
    y_mem = _dot(jnp.concatenate(heads, axis=-1).astype(BF16), wom_ref[...])

    y_mla = _dot(o_ref[...], womla_ref[...])
    x1_ref[...] = _merge_out(x, gsig, y_mla, y_conv, y_mem, wout_ref[...])


def _mixers(x2d, o_attn, mk, mv, batch, seq, w):
    t, d = x2d.shape
    tm = min(256, seq)
    ns = seq // tm
    nb = w["w_b"].shape[1]
    mtok = mk.shape[1]
    row = lambda b, j: (b * ns + j, 0)
    return pl.pallas_call(
        _mixers_kernel,
        grid=(batch, ns),
        in_specs=[pl.BlockSpec((tm, d), row), pl.BlockSpec((tm, MLA_HEADS * V_HEAD), row),
                  pl.BlockSpec((1, mtok, MEM_W), lambda b, j: (b, 0, 0)),
                  pl.BlockSpec((1, mtok, MEM_W), lambda b, j: (b, 0, 0)),
                  _const_spec((1, d)), _const_spec((d, nb)), _const_spec((CONV_W, C_CONV)),
                  _const_spec((1, C_CONV)), _const_spec((1, C_CONV)), _const_spec((1, C_CONV)),
                  _const_spec((C_CONV, d)), _const_spec((1, MEM_HD)), _const_spec((MEM_W, d)),
                  _const_spec((MLA_HEADS * V_HEAD, d)), _const_spec((d, d))],
        out_specs=[pl.BlockSpec((tm, d), row),
                   pl.BlockSpec((1, CONV_W - 1, C_CONV), lambda b, j: (b, 0, 0))],
        out_shape=[jax.ShapeDtypeStruct((t, d), F32), jax.ShapeDtypeStruct((batch, CONV_W - 1, C_CONV), F32)],
        scratch_shapes=[pltpu.VMEM((tm + CONV_TAIL, C_CONV), F32)],
        compiler_params=_params(("parallel", "arbitrary"), 56),
        name="token_mixers",
    )(x2d, o_attn, mk, mv, w["g_attn"], w["w_b"], w["w_dw"], w["b_dw"], w["g_conv_ln"], w["b_conv_ln"],
      w["w_conv_out"], w["g_q_mem"], w["w_o_mem"], w["w_o_mla"], w["w_out"])


def _ffn_prep_kernel(x1_ref, g_ref, wrh_ref, wrl_ref, br_ref, wsg_ref, wsu_ref, wsd_ref,
                     base_ref, h2u_ref, idx_ref, w_ref):
    x1 = x1_ref[...]
    d = x1.shape[-1]
    h2 = _rms(x1, g_ref[...])
    hh, hl = _split(h2)
    wrh = wrh_ref[...]
    logits = _dot_nt(wrh, hh) + _dot_nt(wrh, hl) + _dot_nt(wrl_ref[...], hh)
    scores = _sigmoid(logits)
    val = scores + br_ref[...]
    eio = lax.broadcasted_iota(I32, val.shape, 0).astype(F32)
    idxs, ws = [], []
    for _ in range(TOP_K):
        m = jnp.max(val, axis=0, keepdims=True)
        sel = jnp.min(jnp.where(val == m, eio, float(N_EXPERTS)), axis=0, keepdims=True)
        hit = eio == sel
        ws.append(jnp.sum(jnp.where(hit, scores, 0.0), axis=0, keepdims=True))
        idxs.append(sel)
        val = jnp.where(hit, -jnp.inf, val)
    wk = jnp.concatenate(ws, axis=0)
    idx_ref[...] = jnp.concatenate(idxs, axis=0).astype(I32)
    w_ref[...] = wk / jnp.sum(wk, axis=0, keepdims=True) * ROUTED_SCALE

    a = _silu(_dot(hh, wsg_ref[...])) * _dot(hh, wsu_ref[...])
    base_ref[...] = x1 + _dot(a.astype(BF16), wsd_ref[...])

    bits = lax.bitcast_convert_type(hh.astype(F32), U32)
    lo = lax.shift_right_logical(bits[:, :d // 2], jnp.uint32(16))
    hi = bits[:, d // 2:] & jnp.uint32(0xFFFF0000)
    h2u_ref[...] = hi | lo


def _ffn_prep(x1, w, tm):
    t, d = x1.shape
    ds = w["w_s_gate"].shape[1]
    row = lambda i: (i, 0)
    col = lambda i: (0, i)
    return pl.pallas_call(
        _ffn_prep_kernel,
        grid=(t // tm,),
        in_specs=[pl.BlockSpec((tm, d), row), _const_spec((1, d)), _const_spec((N_EXPERTS, d)),
                  _const_spec((N_EXPERTS, d)), _const_spec((N_EXPERTS, 1)), _const_spec((d, ds)),
                  _const_spec((d, ds)), _const_spec((ds, d))],
        out_specs=[pl.BlockSpec((tm, d), row), pl.BlockSpec((tm, d // 2), row),
                   pl.BlockSpec((TOP_K, tm), col), pl.BlockSpec((TOP_K, tm), col)],
        out_shape=[jax.ShapeDtypeStruct((t, d), F32), jax.ShapeDtypeStruct((t, d // 2), U32),
                   jax.ShapeDtypeStruct((TOP_K, t), I32), jax.ShapeDtypeStruct((TOP_K, t), F32)],
        compiler_params=_params(("parallel",), 40),
        name="ffn_prep",
    )(x1, w["g_ffn"], w["w_r_hi"], w["w_r_lo"], w["b_router"], w["w_s_gate"], w["w_s_up"], w["w_s_down"])


def _route_kernel(idx_ref, tri_ref, dest_ref, blk_ref, nused_ref, cnt_scr, run_scr, *, blk):
    p = pl.program_id(0)
    i = pl.program_id(1)
    tm = idx_ref.shape[1]
    idx = idx_ref[...]
    eio = lax.broadcasted_iota(I32, (N_EXPERTS, tm), 0)
    onehot = jnp.zeros((N_EXPERTS, tm), F32)
    for k in range(TOP_K):
        onehot = onehot + (eio == idx[k:k + 1, :]).astype(F32)

    @pl.when((p == 0) & (i == 0))
    def _():
        cnt_scr[...] = jnp.zeros(cnt_scr.shape, F32)

    @pl.when(p == 0)
    def _():
        cnt_scr[...] += jnp.sum(onehot, axis=1, keepdims=True)

    @pl.when((p == 1) & (i == 0))
    def _():
        cnt = cnt_scr[...]
        padded = jnp.ceil(cnt * (1.0 / blk)) * blk
        r = lax.broadcasted_iota(I32, (N_EXPERTS, N_EXPERTS), 0)
        c = lax.broadcasted_iota(I32, (N_EXPERTS, N_EXPERTS), 1)
        pb = jnp.broadcast_to(padded, (N_EXPERTS, N_EXPERTS))
        prow = jnp.sum(jnp.where(r == c, pb, 0.0), axis=0, keepdims=True)
        prb = jnp.broadcast_to(prow, (N_EXPERTS, N_EXPERTS))
        pstart = jnp.sum(jnp.where(c < r, prb, 0.0), axis=1, keepdims=True)
        run_scr[...] = pstart
        pend = pstart + padded
        nb = blk_ref.shape[1]
        bstart = (lax.broadcasted_iota(I32, (N_EXPERTS, nb), 1) * blk).astype(F32)
        be = jnp.sum((jnp.broadcast_to(pend, (N_EXPERTS, nb)) <= bstart).astype(F32), axis=0, keepdims=True)
        blk_ref[...] = jnp.minimum(be, N_EXPERTS - 1.0).astype(I32)
        total = jnp.sum(padded, axis=0, keepdims=True) * (1.0 / blk)
        nused_ref[...] = jnp.broadcast_to(total, nused_ref.shape).astype(I32)

    @pl.when(p == 1)
    def _():
        within = _dot(onehot.astype(BF16), tri_ref[...])
        slot = run_scr[...] + within
        rows = []
        for k in range(TOP_K):
            rows.append(jnp.sum(jnp.where(eio == idx[k:k + 1, :], slot, 0.0), axis=0, keepdims=True))
        dest_ref[...] = jnp.concatenate(rows, axis=0).astype(I32)
        run_scr[...] += jnp.sum(onehot, axis=1, keepdims=True)


def _route(idx_pad, blk, nb_pad):
    tpad = idx_pad.shape[1]
    tm = 512
    tri = (lax.broadcasted_iota(I32, (tm, tm), 0) < lax.broadcasted_iota(I32, (tm, tm), 1)).astype(BF16)
    return pl.pallas_call(
        functools.partial(_route_kernel, blk=blk),
        grid=(2, tpad // tm),
        in_specs=[pl.BlockSpec((TOP_K, tm), lambda p, i: (0, i)), _const_spec((tm, tm))],
        out_specs=[pl.BlockSpec((TOP_K, tm), lambda p, i: (0, i * p)), _const_spec((1, nb_pad)),
                   _const_spec((1, LANES))],
        out_shape=[jax.ShapeDtypeStruct((TOP_K, tpad), I32), jax.ShapeDtypeStruct((1, nb_pad), I32),
                   jax.ShapeDtypeStruct((1, LANES), I32)],
        scratch_shapes=[pltpu.VMEM((N_EXPERTS, 1), F32), pltpu.VMEM((N_EXPERTS, 1), F32)],
        compiler_params=_params(("arbitrary", "arbitrary"), 32),
        name="moe_route",
    )(idx_pad, tri)


def _dispatch_kernel(dest_ref, h2u_ref, xs_in_ref, xs_ref, sem):
    del xs_in_ref
    i = pl.program_id(0)
    tm = dest_ref.shape[1]

    def body(t, carry):
        src = h2u_ref.at[pl.ds(i * tm + t, 1)]
        for k in range(TOP_K):
            pltpu.make_async_copy(src, xs_ref.at[pl.ds(dest_ref[k, t], 1)], sem).start()
        return carry

    lax.fori_loop(0, tm, body, 0)

    def drain(t, carry):
        src = h2u_ref.at[pl.ds(i * tm + t, 1)]
        for k in range(TOP_K):
            pltpu.make_async_copy(src, xs_ref.at[pl.ds(dest_ref[k, t], 1)], sem).wait()
        return carry

    lax.fori_loop(0, tm, drain, 0)


def _dispatch(dest, h2u, xs, tm):
    t = h2u.shape[0]
    return pl.pallas_call(
        _dispatch_kernel,
        grid=(t // tm,),
        in_specs=[pl.BlockSpec((TOP_K, tm), lambda i: (0, i), memory_space=pltpu.SMEM),
                  pl.BlockSpec(memory_space=pl.ANY), pl.BlockSpec(memory_space=pl.ANY)],
        out_specs=pl.BlockSpec(memory_space=pl.ANY),
        out_shape=jax.ShapeDtypeStruct(xs.shape, xs.dtype),
        scratch_shapes=[pltpu.SemaphoreType.DMA(())],
        input_output_aliases={2: 0},
        compiler_params=pltpu.CompilerParams(dimension_semantics=("arbitrary",), has_side_effects=True),
        name="moe_dispatch",
    )(dest, h2u, xs)


def _expert_kernel(be_ref, nu_ref, xs_ref, wg_ref, wu_ref, wd_ref, y_ref):
    b = pl.program_id(0)
    half = xs_ref.shape[1]

    @pl.when(b < nu_ref[0])
    def _():
        word = xs_ref[...]
        lo = lax.bitcast_convert_type(lax.shift_left(word, jnp.uint32(16)), F32).astype(BF16)
        hi = lax.bitcast_convert_type(word & jnp.uint32(0xFFFF0000), F32).astype(BF16)
        g = _dot(lo, wg_ref[0, :half, :]) + _dot(hi, wg_ref[0, half:, :])
        u = _dot(lo, wu_ref[0, :half, :]) + _dot(hi, wu_ref[0, half:, :])
        y_ref[...] = _dot((_silu(g) * u).astype(BF16), wd_ref[0])

    @pl.when(b >= nu_ref[0])
    def _():
        y_ref[...] = jnp.zeros(y_ref.shape, F32)


def _experts(block_e, nused, xs, wg, wu, wd, blk):
    nslots, half = xs.shape
    d = 2 * half
    de = wg.shape[2]
    grid_spec = pltpu.PrefetchScalarGridSpec(
        num_scalar_prefetch=2,
        grid=(nslots // blk,),
        in_specs=[pl.BlockSpec((blk, half), lambda b, be, nu: (b, 0)),
                  pl.BlockSpec((1, d, de), lambda b, be, nu: (be[b], 0, 0)),
                  pl.BlockSpec((1, d, de), lambda b, be, nu: (be[b], 0, 0)),
                  pl.BlockSpec((1, de, d), lambda b, be, nu: (be[b], 0, 0))],
        out_specs=pl.BlockSpec((blk, d), lambda b, be, nu: (b, 0)),
    )
    return pl.pallas_call(
        _expert_kernel,
        grid_spec=grid_spec,
        out_shape=jax.ShapeDtypeStruct((nslots, d), F32),
        compiler_params=_params(("arbitrary",), 32),
        name="moe_experts",
    )(block_e, nused, xs, wg, wu, wd)


def _combine_kernel(dcur_ref, dnext_ref, w_ref, base_ref, y_ref, out_ref, buf, sem):
    i = pl.program_id(0)
    n = pl.num_programs(0)
    tm = base_ref.shape[0]

    def copies(dref, slot, t, k):
        return pltpu.make_async_copy(y_ref.at[pl.ds(dref[k, t], 1)], buf.at[slot, pl.ds(k * tm + t, 1)],
                                     sem.at[slot])

    def issue(dref, slot):
        def body(t, carry):
            for k in range(TOP_K):
                copies(dref, slot, t, k).start()
            return carry
        lax.fori_loop(0, tm, body, 0)

    @pl.when(i == 0)
    def _():
        issue(dcur_ref, 0)

    @pl.when(i + 1 < n)
    def _():
        issue(dnext_ref, (i + 1) % 2)

    slot = i % 2

    def drain(t, carry):
        for k in range(TOP_K):
            copies(dcur_ref, slot, t, k).wait()
        return carry

    lax.fori_loop(0, tm, drain, 0)

    acc = base_ref[...]
    w = w_ref[...]
    for k in range(TOP_K):
        acc = acc + w[:, k:k + 1] * buf[slot, k * tm:(k + 1) * tm, :]
    out_ref[...] = acc


def _combine(dest, w_t, base, y, tm):
    t, d = base.shape
    n = t // tm
    return pl.pallas_call(
        _combine_kernel,
        grid=(n,),
        in_specs=[pl.BlockSpec((TOP_K, tm), lambda i: (0, i), memory_space=pltpu.SMEM),
                  pl.BlockSpec((TOP_K, tm), lambda i: (0, jnp.minimum(i + 1, n - 1)), memory_space=pltpu.SMEM),
                  pl.BlockSpec((tm, TOP_K), lambda i: (i, 0)), pl.BlockSpec((tm, d), lambda i: (i, 0)),
                  pl.BlockSpec(memory_space=pl.ANY)],
        out_specs=pl.BlockSpec((tm, d), lambda i: (i, 0)),
        out_shape=jax.ShapeDtypeStruct((t, d), F32),
        scratch_shapes=[pltpu.VMEM((2, TOP_K * tm, d), F32), pltpu.SemaphoreType.DMA((2,))],
        compiler_params=_params(("arbitrary",), 32),
        name="moe_combine",
    )(dest, dest, w_t, base, y)


def _sample_in_kernel(x_ref, ga_ref, wa_ref, gq_ref, wuq_ref, gqm_ref, gkv_ref, wuk_ref, e32_ref, gkm_ref,
                      gkn_ref, cos_ref, sa_ref, sb_ref, wb_ref, st_ref, wdw_ref, bdw_ref, gln_ref, bln_ref,
                      wco_ref, gqmem_ref,
                      c_ref, kr_ref, u_ref, qmn_ref, gsig_ref, yconv_ref, qabs_ref, qrope_ref, snew_ref):
    h, q_raw, k_raw, c, _, kr = _qkv_core(x_ref[...], ga_ref[...], wa_ref[...], gq_ref[...], wuq_ref[...],
                                          gqm_ref[...], gkv_ref[...], wuk_ref[...], e32_ref[...], gkm_ref[...])
    c_ref[...] = c
    kr_ref[...] = kr
    cos, sa, sb = cos_ref[...], sa_ref[...], sb_ref[...]
    for hd in range(MLA_HEADS):
        sl = slice(hd * HEAD_PAD, (hd + 1) * HEAD_PAD)
        qh = _rope128(_head_norm(q_raw[:, sl], gqm_ref[:, sl]), cos, sa, sb)
        kh = _rope128(_head_norm(k_raw[:, sl], gkm_ref[:, sl]), cos, sa, sb)
        snew_ref[:, hd:hd + 1] = jnp.sum(qh * kh, axis=-1, keepdims=True) * MLA_SCALE
        qrope_ref[hd] = qh[:, QK_NOPE:QK_HEAD]
        g_hi, g_lo = _split(qh * gkn_ref[:, sl])
        wk = wuk_ref[:, sl]
        qabs_ref[hd] = _dot_nt(g_hi, wk) + _dot_nt(g_lo, wk)

    pb = _dot(h, wb_ref[...])
    u = pb[:, :C_CONV] * _sigmoid(pb[:, C_CONV:2 * C_CONV])
    u_ref[...] = u
    for hd in range(MEM_HEADS):
        sl = slice(2 * C_CONV + hd * MEM_HD, 2 * C_CONV + (hd + 1) * MEM_HD)
        qmn_ref[:, hd * MEM_HD:(hd + 1) * MEM_HD] = _rms(pb[:, sl], gqmem_ref[...])
    gsig_ref[...] = _sigmoid(pb[:, 2 * C_CONV + MEM_W:])

    y = bdw_ref[...] + wdw_ref[CONV_W - 1:CONV_W, :] * u
    for tap in range(CONV_W - 1):
        y = y + wdw_ref[tap:tap + 1, :] * st_ref[tap]
    yconv_ref[...] = _dot(_ln_silu(y, gln_ref[...], bln_ref[...]).astype(BF16), wco_ref[...])


def _sample_in(xs2d, state_t, w, tabs):
    db, d = xs2d.shape
    ins = [xs2d, w["g_attn"], w["w_a"], w["g_q_lora"], w["w_uq_p"], w["g_q_mla_p"], w["g_kv_lora"], w["w_uk_p"],
           w["e32"], w["g_k_mla_p"], w["g_k_nope_p"], *tabs, w["w_b"], state_t, w["w_dw"], w["b_dw"],
           w["g_conv_ln"], w["b_conv_ln"], w["w_conv_out"], w["g_q_mem"]]
    outs = [jax.ShapeDtypeStruct((db, KV_LORA), F32), jax.ShapeDtypeStruct((db, QK_ROPE), F32),
            jax.ShapeDtypeStruct((db, C_CONV), F32), jax.ShapeDtypeStruct((db, MEM_W), F32),
            jax.ShapeDtypeStruct((db, N_BRANCH * d), F32), jax.ShapeDtypeStruct((db, d), F32),
            jax.ShapeDtypeStruct((MLA_HEADS, db, KV_LORA), F32), jax.ShapeDtypeStruct((MLA_HEADS, db, QK_ROPE), F32),
            jax.ShapeDtypeStruct((db, MLA_HEADS), F32)]
    return pl.pallas_call(
        _sample_in_kernel,
        grid=(1,),
        in_specs=[_const_spec(a.shape) for a in ins],
        out_specs=[_const_spec(o.shape) for o in outs],
        out_shape=outs,
        compiler_params=_params(("arbitrary",), 56),
        name="sample_in",
    )(*ins)


def _sample_mla_kernel(pt_ref, qabs_ref, qrope_ref, wukt_ref, sel_ref, g1_ref, g2_ref, cost_ref, sint_ref,
                       lat_ref, kro_ref, ctx_ref, m_ref, l_ref, cbuf, krbuf, lhs, sem_c, sem_k, *, n_chunks, ch):
    b = pl.program_id(0)
    nseq = pl.num_programs(0)
    page = cbuf.shape[2]
    pc = ch * page
    nrow = MLA_HEADS * QK_NOPE

    def page_copies(seq, chunk, slot, p):
        pid = pt_ref[seq, chunk * ch + p]
        return (pltpu.make_async_copy(lat_ref.at[pid], cbuf.at[slot, p], sem_c.at[slot]),
                pltpu.make_async_copy(kro_ref.at[pid], krbuf.at[slot, p], sem_k.at[slot]))

    def issue(seq, chunk, slot):
        for p in range(ch):
            a, k = page_copies(seq, chunk, slot, p)
            a.start()
            k.start()

    @pl.when(b == 0)
    def _():
        lhs[0:nrow, :] = wukt_ref[...]
        issue(0, 0, 0)

    qa = jnp.concatenate([qabs_ref[0], jnp.zeros((16 - MLA_HEADS, KV_LORA), F32)], axis=0)
    lhs[nrow:nrow + 16, :] = qa.astype(BF16)
    qr = qrope_ref[0].astype(BF16)
    g1 = g1_ref[...]
    g2 = g2_ref[...]
    sel = sel_ref[...]

    def chunk_body(c, carry):
        m_prev, l_prev, ctx = carry
        slot = c % 2

        @pl.when(c + 1 < n_chunks)
        def _():
            issue(b, c + 1, 1 - slot)

        @pl.when((c + 1 == n_chunks) & (b + 1 < nseq))
        def _():
            issue(b + 1, 0, 1 - slot)

        for p in range(ch):
            a, k = page_copies(b, c, slot, p)
            a.wait()
            k.wait()

        cb = cbuf[slot].reshape(pc, KV_LORA).astype(BF16)
        kt = _dot_nt(lhs[...], cb)
        ssn = jnp.concatenate(
            [jnp.sum(jnp.square(kt[h * QK_NOPE:(h + 1) * QK_NOPE]), axis=0, keepdims=True)
             for h in range(MLA_HEADS)], axis=0)
        s_nope = kt[nrow:nrow + MLA_HEADS]

        kr_hi, kr_lo = _split(krbuf[slot].reshape(pc, QK_ROPE))
        t1 = _dot_nt(sel, kr_hi) + _dot_nt(sel, kr_lo)
        krt = t1[:QK_ROPE]
        ssr = jnp.sum(krt * krt, axis=0, keepdims=True)
        roped = krt * g1 * cost_ref[c] + t1[QK_ROPE:] * g2 * sint_ref[c]
        s_rope = _dot(qr, roped.astype(BF16))

        rinv = lax.rsqrt((ssn + ssr) * (1.0 / QK_HEAD) + EPS)
        s = (s_nope + s_rope) * rinv * MLA_SCALE
        m_new = jnp.maximum(m_prev, jnp.max(s, axis=-1, keepdims=True))
        alpha = jnp.exp(m_prev - m_new)
        pexp = jnp.exp(s - m_new)
        l_new = alpha * l_prev + jnp.sum(pexp, axis=-1, keepdims=True)
        ctx_new = alpha * ctx + _dot(pexp.astype(BF16), cb)
        return m_new, l_new, ctx_new

    init = (jnp.full((MLA_HEADS, 1), NEG_BIG, F32), jnp.zeros((MLA_HEADS, 1), F32),
            jnp.zeros((MLA_HEADS, KV_LORA), F32))
    m_fin, l_fin, ctx = lax.fori_loop(0, n_chunks, chunk_body, init)
    ctx_ref[0] = ctx
    m_ref[0] = jnp.broadcast_to(m_fin, (MLA_HEADS, LANES))
    l_ref[0] = jnp.broadcast_to(l_fin, (MLA_HEADS, LANES))


def _sample_mla(page_table, qabs, qrope, lat, kro, w, ch):
    db, n_pages = page_table.shape
    page = lat.shape[1]
    n_chunks = n_pages // ch
    pc = ch * page
    nrow = MLA_HEADS * QK_NOPE
    grid_spec = pltpu.PrefetchScalarGridSpec(
        num_scalar_prefetch=1,
        grid=(db,),
        in_specs=[pl.BlockSpec((1, MLA_HEADS, KV_LORA), lambda b, pt: (b, 0, 0)),
                  pl.BlockSpec((1, MLA_HEADS, QK_ROPE), lambda b, pt: (b, 0, 0)),
                  pl.BlockSpec((nrow, KV_LORA), lambda b, pt: (0, 0)),
                  pl.BlockSpec((2 * QK_ROPE, QK_ROPE), lambda b, pt: (0, 0)),
                  pl.BlockSpec((QK_ROPE, 1), lambda b, pt: (0, 0)),
                  pl.BlockSpec((QK_ROPE, 1), lambda b, pt: (0, 0)),
                  pl.BlockSpec((n_chunks, QK_ROPE, pc), lambda b, pt: (0, 0, 0)),
                  pl.BlockSpec((n_chunks, QK_ROPE, pc), lambda b, pt: (0, 0, 0)),
                  pl.BlockSpec(memory_space=pl.ANY), pl.BlockSpec(memory_space=pl.ANY)],
        out_specs=[pl.BlockSpec((1, MLA_HEADS, KV_LORA), lambda b, pt: (b, 0, 0)),
                   pl.BlockSpec((1, MLA_HEADS, LANES), lambda b, pt: (b, 0, 0)),
                   pl.BlockSpec((1, MLA_HEADS, LANES), lambda b, pt: (b, 0, 0))],
        scratch_shapes=[pltpu.VMEM((2, ch, page, KV_LORA), F32), pltpu.VMEM((2, ch, page, QK_ROPE), F32),
                        pltpu.VMEM((nrow + 16, KV_LORA), BF16),
                        pltpu.SemaphoreType.DMA((2,)), pltpu.SemaphoreType.DMA((2,))],
    )
    return pl.pallas_call(
        functools.partial(_sample_mla_kernel, n_chunks=n_chunks, ch=ch),
        grid_spec=grid_spec,
        out_shape=[jax.ShapeDtypeStruct((db, MLA_HEADS, KV_LORA), F32),
                   jax.ShapeDtypeStruct((db, MLA_HEADS, LANES), F32),
                   jax.ShapeDtypeStruct((db, MLA_HEADS, LANES), F32)],
        compiler_params=_params(("arbitrary",), 56),
        name="sample_mla",
    )(page_table, qabs, qrope, w["w_uk_t"], w["rope_sel"], w["g_kr1"], w["g_kr2"], w["cos_t"], w["sin_t"],
      lat, kro)


def _sample_mem_kernel(q_ref, k_ref, v_ref, o_ref):
    q = q_ref[0]
    prod = k_ref[0] * q
    v = v_ref[0]
    for h in range(MEM_HEADS):
        sl = slice(h * MEM_HD, (h + 1) * MEM_HD)
        s = jnp.sum(prod[:, sl], axis=-1, keepdims=True) * MEM_SCALE
        e = jnp.exp(s - jnp.max(s, axis=0, keepdims=True))
        p = e / jnp.sum(e, axis=0, keepdims=True)
        o_ref[0, :, sl] = jnp.sum(p * v[:, sl], axis=0, keepdims=True)


def _sample_mem(qmn, mk, mv):
    db, mtok, _ = mk.shape
    q3 = qmn.reshape(db, 1, MEM_W)
    return pl.pallas_call(
        _sample_mem_kernel,
        grid=(db,),
        in_specs=[pl.BlockSpec((1, 1, MEM_W), lambda b: (b, 0, 0)),
                  pl.BlockSpec((1, mtok, MEM_W), lambda b: (b, 0, 0)),
                  pl.BlockSpec((1, mtok, MEM_W), lambda b: (b, 0, 0))],
        out_specs=pl.BlockSpec((1, 1, MEM_W), lambda b: (b, 0, 0)),
        out_shape=jax.ShapeDtypeStruct((db, 1, MEM_W), F32),
        compiler_params=_params(("parallel",), 32),
        name="sample_mem",
    )(q3, mk, mv).reshape(db, MEM_W)


def _sample_out_kernel(x_ref, ctx_ref, m_ref, l_ref, snew_ref, c_ref, wuv_ref, womla_ref, omem_ref, wom_ref,
                       gsig_ref, yconv_ref, wout_ref, x1_ref):
    c_new = c_ref[...]
    m = m_ref[...]
    l = l_ref[...]
    s_new = snew_ref[...]
    d = x_ref.shape[-1]
    y_mla = jnp.zeros((x_ref.shape[0], d), F32)
    for h in range(MLA_HEADS):
        mh, lh, sh = m[:, h:h + 1], l[:, h:h + 1], s_new[:, h:h + 1]
        m_fin = jnp.maximum(mh, sh)
        a_old = jnp.exp(mh - m_fin)
        a_new = jnp.exp(sh - m_fin)
        denom = lh * a_old + a_new
        ctx = (ctx_ref[h] * a_old + a_new * c_new) / denom
        o_h = _dot(ctx.astype(BF16), wuv_ref[h])
        y_mla = y_mla + _dot(o_h.astype(BF16), womla_ref[h])
    y_mem = _dot(omem_ref[...].astype(BF16), wom_ref[...])
    x1_ref[...] = _merge_out(x_ref[...], gsig_ref[...], y_mla, yconv_ref[...], y_mem, wout_ref[...])


def _sample_out(xs2d, ctx_t, m, l, s_new, c_s, o_mem, gsig, y_conv, w):
    ins = [xs2d, ctx_t, m, l, s_new, c_s, w["w_uv_h"], w["w_o_mla_h"], o_mem, w["w_o_mem"], gsig, y_conv,
           w["w_out"]]
    return pl.pallas_call(
        _sample_out_kernel,
        grid=(1,),
        in_specs=[_const_spec(a.shape) for a in ins],
        out_specs=_const_spec(xs2d.shape),
        out_shape=jax.ShapeDtypeStruct(xs2d.shape, F32),
        compiler_params=_params(("arbitrary",), 40),
        name="sample_out",
    )(*ins)


def _prep_weights(g_attn_norm, w_in, g_q_lora, w_uq, g_q_mla, g_kv_lora, w_uk, w_uv, g_k_mla, w_o_mla, w_dw, b_dw,
                  g_conv_ln, b_conv_ln, w_conv_out, g_mem_norm, w_mem_kv, g_q_mem, g_k_mem, w_o_mem, w_out,
                  g_ffn_norm, w_router, b_router, w_e_gate, w_e_up, w_e_down, w_s_gate, w_s_up, w_s_down):
    row = lambda g: g.reshape(1, -1).astype(F32)
    e32 = jnp.zeros((QK_ROPE, MLA_HEADS, HEAD_PAD), F32)
    e32 = e32.at[jnp.arange(QK_ROPE), :, QK_NOPE + jnp.arange(QK_ROPE)].set(1.0)
    half = QK_ROPE // 2
    ident = jnp.eye(QK_ROPE, dtype=F32)
    rot = jnp.concatenate([-ident[half:], ident[:half]], axis=0)
    g_rope = g_k_mla[QK_NOPE:]
    w_r_hi = w_router.T.astype(BF16)
    w_r_lo = (w_router.T - w_r_hi.astype(F32)).astype(BF16)
    g_k_nope = jnp.concatenate([g_k_mla[:QK_NOPE], jnp.zeros((QK_ROPE,), F32)])
    return {
        "g_attn": row(g_attn_norm),
        "w_a": w_in[:, :QKV_COLS].astype(BF16),
        "w_b": w_in[:, QKV_COLS:].astype(BF16),
        "g_q_lora": row(g_q_lora),
        "w_uq_p": _pad_heads(w_uq, QK_HEAD).astype(BF16),
        "g_q_mla_p": _pad_heads(jnp.tile(g_q_mla, MLA_HEADS).reshape(1, -1), QK_HEAD),
        "g_kv_lora": row(g_kv_lora),
        "w_uk_p": _pad_heads(w_uk.reshape(KV_LORA, MLA_HEADS * QK_NOPE), QK_NOPE).astype(BF16),
        "e32": e32.reshape(QK_ROPE, MLA_HEADS * HEAD_PAD).astype(BF16),
        "g_k_mla_p": _pad_heads(jnp.tile(g_k_mla, MLA_HEADS).reshape(1, -1), QK_HEAD),
        "g_k_nope_p": _pad_heads(jnp.tile(g_k_nope, MLA_HEADS).reshape(1, -1), QK_HEAD),
        "w_uv_p": _pad_heads(w_uv.reshape(KV_LORA, MLA_HEADS * V_HEAD), V_HEAD).astype(BF16),
        "w_uv_h": w_uv.transpose(1, 0, 2).astype(BF16),
        "w_uk_t": w_uk.reshape(KV_LORA, MLA_HEADS * QK_NOPE).T.astype(BF16),
        "rope_sel": jnp.concatenate([ident, rot], axis=0).astype(BF16),
        "g_kr1": g_rope.reshape(QK_ROPE, 1),
        "g_kr2": jnp.concatenate([g_rope[half:], g_rope[:half]]).reshape(QK_ROPE, 1),
        "w_o_mla": w_o_mla.astype(BF16),
        "w_o_mla_h": w_o_mla.reshape(MLA_HEADS, V_HEAD, -1).astype(BF16),
        "w_dw": w_dw.astype(F32), "b_dw": row(b_dw), "g_conv_ln": row(g_conv_ln), "b_conv_ln": row(b_conv_ln),
        "w_conv_out": w_conv_out.astype(BF16),
        "g_mem_norm": row(g_mem_norm), "w_mem_kv": w_mem_kv.astype(BF16),
        "g_q_mem": row(g_q_mem), "g_k_mem": row(g_k_mem),
        "w_o_mem": w_o_mem.astype(BF16), "w_out": w_out.astype(BF16),
        "g_ffn": row(g_ffn_norm), "w_r_hi": w_r_hi, "w_r_lo": w_r_lo,
        "b_router": b_router.reshape(N_EXPERTS, 1).astype(F32),
        "w_e_gate": w_e_gate.astype(BF16), "w_e_up": w_e_up.astype(BF16), "w_e_down": w_e_down.astype(BF16),
        "w_s_gate": w_s_gate.astype(BF16), "w_s_up": w_s_up.astype(BF16), "w_s_down": w_s_down.astype(BF16),
    }


def _layer(xp, xs, mem_prompt, cache_latent, cache_krope, state_conv, cache_mem_k, cache_mem_v, page_table, w):
    b, s, d = xp.shape
    db, ds, _ = xs.shape
    assert ds == 1, "the sample group decodes one token per sequence"
    n_pages, page = page_table.shape[1], cache_latent.shape[1]
    past = n_pages * page
    mtok = mem_prompt.shape[1]
    xp2 = xp.reshape(b * s, d)
    xs2 = xs.reshape(db * ds, d)

    mk_p, mv_p = _mem_kv(mem_prompt.reshape(b * mtok, d), w["g_mem_norm"], w["w_mem_kv"], w["g_k_mem"])
    q, k, v, c_p, kr_p = _qkv(xp2, s, w, _rope_tables(jnp.arange(s)))
    o_attn = _flash(q, k, v, b, s)
    x1_p, conv_p = _mixers(xp2, o_attn, mk_p.reshape(b, mtok, MEM_W), mv_p.reshape(b, mtok, MEM_W), b, s, w)

    ch = math.gcd(n_pages, 16)
    pc = ch * page
    half = QK_ROPE // 2
    inv = ROPE_THETA ** (-jnp.arange(half, dtype=F32) / half)
    ang = inv[:, None] * jnp.arange(past, dtype=F32)[None, :]
    tab = lambda a: jnp.concatenate([a, a], axis=0).reshape(QK_ROPE, past // pc, pc).transpose(1, 0, 2)
    ws = dict(w, cos_t=tab(jnp.cos(ang)), sin_t=tab(jnp.sin(ang)))
    tabs_s = _rope_tables(jnp.full((db,), past, I32))
    (c_s, kr_s, u_s, qmn, gsig_s, yconv_s, qabs, qrope, s_new) = _sample_in(
        xs2, state_conv.transpose(1, 0, 2), w, tabs_s)
    ctx, m_s, l_s = _sample_mla(page_table, qabs.transpose(1, 0, 2), qrope.transpose(1, 0, 2),
                                cache_latent, cache_krope, ws, ch)
    o_mem = _sample_mem(qmn, cache_mem_k.reshape(db, mtok, MEM_W), cache_mem_v.reshape(db, mtok, MEM_W))
    x1_s = _sample_out(xs2, ctx.transpose(1, 0, 2), m_s[:, :, 0], l_s[:, :, 0], s_new, c_s, o_mem, gsig_s,
                       yconv_s, w)

    base_p, h2u_p, idx_p, wk_p = _ffn_prep(x1_p, w, min(512, b * s))
    base_s, h2u_s, idx_s, wk_s = _ffn_prep(x1_s, w, db)
    tp, tsm = b * s, db
    t_all = tp + tsm
    blk = 256
    route_tile = 512
    t_pad = -(-t_all // route_tile) * route_tile
    idx_all = jnp.concatenate([idx_p, idx_s, jnp.full((TOP_K, t_pad - t_all), N_EXPERTS, I32)], axis=1)
    n_blocks = -(-t_all * TOP_K // blk) + N_EXPERTS
    nb_pad = -(-n_blocks // LANES) * LANES
    dest, block_e, nused = _route(idx_all, blk, nb_pad)
    xs_sorted = jnp.zeros((n_blocks * blk, d // 2), U32)
    xs_sorted = _dispatch(dest[:, :tp], h2u_p, xs_sorted, min(256, tp))
    xs_sorted = _dispatch(dest[:, tp:t_all], h2u_s, xs_sorted, tsm)
    y = _experts(block_e[0, :n_blocks], nused[0, :1], xs_sorted, w["w_e_gate"], w["w_e_up"], w["w_e_down"], blk)
    yp = _combine(dest[:, :tp], wk_p.T, base_p, y, 128)
    ys = _combine(dest[:, tp:t_all], wk_s.T, base_s, y, min(128, tsm))

    new_conv_s = jnp.concatenate([state_conv[:, 1:], u_s[:, None, :]], axis=1)
    return (yp.reshape(b, s, d), ys.reshape(db, ds, d), c_p.reshape(b, s, KV_LORA), kr_p.reshape(b, s, QK_ROPE),
            conv_p, mk_p.reshape(b, mtok, MEM_HEADS, MEM_HD), mv_p.reshape(b, mtok, MEM_HEADS, MEM_HD),
            c_s.reshape(db, ds, KV_LORA), kr_s.reshape(db, ds, QK_ROPE), new_conv_s)


def kernel(x_prompt, x_sample, mem_prompt, cache_latent, cache_krope, state_conv, cache_mem_k, cache_mem_v,
           page_table, g_attn_norm, w_in, g_q_lora, w_uq, g_q_mla, g_kv_lora, w_uk, w_uv, g_k_mla, w_o_mla, w_dw,
           b_dw, g_conv_ln, b_conv_ln, w_conv_out, g_mem_norm, w_mem_kv, g_q_mem, g_k_mem, w_o_mem, w_out,
           g_ffn_norm, w_router, b_router, w_e_gate, w_e_up, w_e_down, w_s_gate, w_s_up, w_s_down):
    depth = w_in.shape[0]
    params = (g_attn_norm, w_in, g_q_lora, w_uq, g_q_mla, g_kv_lora, w_uk, w_uv, g_k_mla, w_o_mla, w_dw, b_dw,
              g_conv_ln, b_conv_ln, w_conv_out, g_mem_norm, w_mem_kv, g_q_mem, g_k_mem, w_o_mem, w_out,
              g_ffn_norm, w_router, b_router, w_e_gate, w_e_up, w_e_down, w_s_gate, w_s_up, w_s_down)
    xp, xs = x_prompt, x_sample
    per_layer = []
    for layer in range(depth):
        w = _prep_weights(*[p[layer] for p in params])
        outs = _layer(xp, xs, mem_prompt, cache_latent[layer], cache_krope[layer], state_conv[layer],
                      cache_mem_k[layer], cache_mem_v[layer], page_table, w)
        xp, xs = outs[0], outs[1]
        per_layer.append(outs[2:])
    stacked = tuple(jnp.stack([pl_[i] for pl_ in per_layer]) for i in range(8))
    return (xp, xs) + stacked
```

```python
import functools
import math

import jax
import jax.numpy as jnp
from jax import lax
from jax.experimental import pallas as pl
from jax.experimental.pallas import tpu as pltpu

F32 = jnp.float32
BF16 = jnp.bfloat16
I32 = jnp.int32
U32 = jnp.uint32

MLA_HEADS = 8
QK_NOPE = 64
QK_ROPE = 32
QK_HEAD = QK_NOPE + QK_ROPE
V_HEAD = 64
Q_LORA = 256
KV_LORA = 256
ROPE_THETA = 10000.0
MLA_SCALE = QK_HEAD ** -0.5
C_CONV = 512
CONV_W = 31
MEM_HEADS = 4
MEM_HD = 128
MEM_W = MEM_HEADS * MEM_HD
MEM_SCALE = MEM_HD ** -0.5
N_BRANCH = 3
N_EXPERTS = 64
TOP_K = 8
D_EXPERT = 256
ROUTED_SCALE = 2.5
EPS = 1e-6
NEG_BIG = -1e30
Q_PRESCALE = MLA_SCALE * math.log2(math.e)

LANES = 128
HEAD_PAD = LANES
QKV_COLS = Q_LORA + KV_LORA + QK_ROPE
CONV_TAIL = 32
MIB = 1024 * 1024


def _dot(a, b):
    return jnp.dot(a, b, preferred_element_type=F32)


def _dot_nt(a, b):
    return lax.dot_general(a, b, (((1,), (1,)), ((), ())), preferred_element_type=F32)


def _split(x):
    hi = x.astype(BF16)
    lo = (x - hi.astype(F32)).astype(BF16)
    return hi, lo


def _rms(x, g):
    return x * lax.rsqrt(jnp.mean(x * x, axis=-1, keepdims=True) + EPS) * g


def _head_norm(x, g):
    ss = jnp.sum(x * x, axis=-1, keepdims=True)
    return x * lax.rsqrt(ss * (1.0 / QK_HEAD) + EPS) * g


def _rope128(n, cos, sa, sb):
    return n * cos + pltpu.roll(n, LANES - QK_ROPE // 2, 1) * sa + pltpu.roll(n, QK_ROPE // 2, 1) * sb


def _sigmoid(x):
    return jax.nn.sigmoid(x)


def _silu(x):
    return x * jax.nn.sigmoid(x)


def _const_spec(shape):
    nd = len(shape)
    return pl.BlockSpec(shape, lambda *_: (0,) * nd)


def _params(sem, vmem_mib):
    return pltpu.CompilerParams(dimension_semantics=sem, vmem_limit_bytes=vmem_mib * MIB)


def _pad_heads(w, used):
    lead = w.shape[:-1]
    w = w.reshape(lead + (MLA_HEADS, used))
    w = jnp.pad(w, [(0, 0)] * len(lead) + [(0, 0), (0, HEAD_PAD - used)])
    return w.reshape(lead + (MLA_HEADS * HEAD_PAD,))


def _rope_tables(pos):
    half = QK_ROPE // 2
    inv = ROPE_THETA ** (-jnp.arange(half, dtype=F32) / half)
    ang = pos.astype(F32)[:, None] * inv[None, :]
    cos, sin = jnp.cos(ang), jnp.sin(ang)
    n = pos.shape[0]
    ones = jnp.ones((n, QK_NOPE), F32)
    zeros = jnp.zeros((n, QK_NOPE), F32)
    tail1 = jnp.ones((n, HEAD_PAD - QK_HEAD), F32)
    tail0 = jnp.zeros((n, HEAD_PAD - QK_HEAD), F32)
    z16 = jnp.zeros((n, half), F32)
    cosb = jnp.concatenate([ones, cos, cos, tail1], axis=1)
    sa = jnp.concatenate([zeros, -sin, z16, tail0], axis=1)
    sb = jnp.concatenate([zeros, z16, sin, tail0], axis=1)
    return cosb, sa, sb


def _mem_kv_kernel(mem_ref, gn_ref, w_ref, gk_ref, k_ref, v_ref):
    m = _rms(mem_ref[...], gn_ref[...]).astype(BF16)
    kv = _dot(m, w_ref[...])
    for h in range(MEM_HEADS):
        sl = slice(h * MEM_HD, (h + 1) * MEM_HD)
        k_ref[:, sl] = _rms(kv[:, sl], gk_ref[...])
    v_ref[...] = kv[:, MEM_W:]


def _mem_kv(mem2d, g_norm, w_kv_bf, g_k):
    n, d = mem2d.shape
    tm = 256
    return pl.pallas_call(
        _mem_kv_kernel,
        grid=(n // tm,),
        in_specs=[pl.BlockSpec((tm, d), lambda i: (i, 0)), _const_spec((1, d)), _const_spec((d, 2 * MEM_W)),
                  _const_spec((1, MEM_HD))],
        out_specs=[pl.BlockSpec((tm, MEM_W), lambda i: (i, 0)), pl.BlockSpec((tm, MEM_W), lambda i: (i, 0))],
        out_shape=[jax.ShapeDtypeStruct((n, MEM_W), F32), jax.ShapeDtypeStruct((n, MEM_W), F32)],
        compiler_params=_params(("parallel",), 32),
        name="mem_kv",
    )(mem2d, g_norm, w_kv_bf, g_k)


def _qkv_core(x, ga, wa, gq, wuq, gqm, gkv, wuk, e32, gkm):
    h = _rms(x, ga).astype(BF16)
    pa = _dot(h, wa)
    q_lat = pa[:, :Q_LORA]
    kv_lat = pa[:, Q_LORA:Q_LORA + KV_LORA]
    kr = pa[:, Q_LORA + KV_LORA:QKV_COLS]
    c_q = _rms(q_lat, gq).astype(BF16)
    q_raw = _dot(c_q, wuq)
    c = _rms(kv_lat, gkv)
    cb = c.astype(BF16)
    kr_hi, kr_lo = _split(kr)
    k_raw = _dot(cb, wuk) + _dot(kr_hi, e32) + _dot(kr_lo, e32)
    return h, q_raw, k_raw, c, cb, kr


def _qkv_kernel(x_ref, ga_ref, wa_ref, gq_ref, wuq_ref, gqm_ref, gkv_ref, wuk_ref, e32_ref, gkm_ref, wuv_ref,
                cos_ref, sa_ref, sb_ref, q_ref, k_ref, v_ref, c_ref, kr_ref):
    _, q_raw, k_raw, c, cb, kr = _qkv_core(x_ref[...], ga_ref[...], wa_ref[...], gq_ref[...], wuq_ref[...],
                                           gqm_ref[...], gkv_ref[...], wuk_ref[...], e32_ref[...], gkm_ref[...])
    c_ref[...] = c
    kr_ref[...] = kr
    lane = lax.broadcasted_iota(I32, (1, MLA_HEADS * HEAD_PAD), 1) % HEAD_PAD
    v_ref[...] = (_dot(cb, wuv_ref[...]) + (lane == V_HEAD).astype(F32)).astype(BF16)
    cos, sa, sb = cos_ref[...], sa_ref[...], sb_ref[...]
    for h in range(MLA_HEADS):
        sl = slice(h * HEAD_PAD, (h + 1) * HEAD_PAD)
        qh = _rope128(_head_norm(q_raw[:, sl], gqm_ref[:, sl]), cos, sa, sb)
        q_ref[:, sl] = (qh * Q_PRESCALE).astype(BF16)
        k_ref[:, sl] = _rope128(_head_norm(k_raw[:, sl], gkm_ref[:, sl]), cos, sa, sb).astype(BF16)


def _qkv(x2d, seq, w, tabs):
    t, d = x2d.shape
    tm = min(512, seq)
    nseq = seq // tm
    hp = MLA_HEADS * HEAD_PAD
    row = lambda i: (i, 0)
    tab = lambda i: (i % nseq, 0)
    return pl.pallas_call(
        _qkv_kernel,
        grid=(t // tm,),
        in_specs=[pl.BlockSpec((tm, d), row), _const_spec((1, d)), _const_spec((d, QKV_COLS)),
                  _const_spec((1, Q_LORA)), _const_spec((Q_LORA, hp)), _const_spec((1, hp)),
                  _const_spec((1, KV_LORA)), _const_spec((KV_LORA, hp)), _const_spec((QK_ROPE, hp)),
                  _const_spec((1, hp)), _const_spec((KV_LORA, hp)),
                  pl.BlockSpec((tm, HEAD_PAD), tab), pl.BlockSpec((tm, HEAD_PAD), tab),
                  pl.BlockSpec((tm, HEAD_PAD), tab)],
        out_specs=[pl.BlockSpec((tm, hp), row), pl.BlockSpec((tm, hp), row), pl.BlockSpec((tm, hp), row),
                   pl.BlockSpec((tm, KV_LORA), row), pl.BlockSpec((tm, QK_ROPE), row)],
        out_shape=[jax.ShapeDtypeStruct((t, hp), BF16), jax.ShapeDtypeStruct((t, hp), BF16),
                   jax.ShapeDtypeStruct((t, hp), BF16), jax.ShapeDtypeStruct((t, KV_LORA), F32),
                   jax.ShapeDtypeStruct((t, QK_ROPE), F32)],
        compiler_params=_params(("parallel",), 48),
        name="qkv_proj",
    )(x2d, w["g_attn"], w["w_a"], w["g_q_lora"], w["w_uq_p"], w["g_q_mla_p"], w["g_kv_lora"], w["w_uk_p"],
      w["e32"], w["g_k_mla_p"], w["w_uv_p"], *tabs)


def _flash_kernel(q_ref, k_ref, v_ref, o_ref, m_scr, acc_scr):
    i = pl.program_id(1)
    j = pl.program_id(2)
    tq = q_ref.shape[0]
    tk = k_ref.shape[0]
    last = i // 2

    @pl.when(j == 0)
    def _():
        m_scr[...] = jnp.full(m_scr.shape, NEG_BIG, F32)
        acc_scr[...] = jnp.zeros(acc_scr.shape, F32)

    def step(nkeys, shift):
        if shift is not None:
            row = lax.broadcasted_iota(I32, (tq, nkeys), 0)
            col = lax.broadcasted_iota(I32, (tq, nkeys), 1)
            keep = col <= row + shift
        for h in range(MLA_HEADS):
            sl = slice(h * HEAD_PAD, (h + 1) * HEAD_PAD)
            s = _dot_nt(q_ref[:, sl], k_ref[0:nkeys, sl])
            if shift is not None:
                s = jnp.where(keep, s, NEG_BIG)
            m_prev = m_scr[h]
            m_new = jnp.maximum(m_prev, jnp.max(s, axis=-1, keepdims=True))
            alpha = jnp.exp2(m_prev - m_new)
            p = jnp.exp2(s - jnp.tile(m_new, (1, nkeys // LANES)))
            acc_scr[h] = alpha * acc_scr[h] + _dot(p.astype(BF16), v_ref[0:nkeys, sl])
            m_scr[h] = m_new

    @pl.when(j < last)
    def _():
        step(tk, None)

    @pl.when((j == last) & (i % 2 == 0))
    def _():
        step(tq, 0)

    @pl.when((j == last) & (i % 2 == 1))
    def _():
        step(tk, tq)

    @pl.when(j == last)
    def _():
        for h in range(MLA_HEADS):
            acc = acc_scr[h]
            o = acc / acc[:, V_HEAD:V_HEAD + 1]
            o_ref[:, h * V_HEAD:(h + 1) * V_HEAD] = o[:, :V_HEAD].astype(BF16)


def _flash(q, k, v, batch, seq):
    t, hp = q.shape
    tq = min(512, seq // 2)
    tk = 2 * tq
    nq, nk = seq // tq, seq // tk
    qmap = lambda b, i, j: (b * nq + i, 0)
    kmap = lambda b, i, j: (b * nk + jnp.minimum(j, i // 2), 0)
    return pl.pallas_call(
        _flash_kernel,
        grid=(batch, nq, nk),
        in_specs=[pl.BlockSpec((tq, hp), qmap), pl.BlockSpec((tk, hp), kmap), pl.BlockSpec((tk, hp), kmap)],
        out_specs=pl.BlockSpec((tq, MLA_HEADS * V_HEAD), qmap),
        out_shape=jax.ShapeDtypeStruct((t, MLA_HEADS * V_HEAD), BF16),
        scratch_shapes=[pltpu.VMEM((MLA_HEADS, tq, LANES), F32), pltpu.VMEM((MLA_HEADS, tq, HEAD_PAD), F32)],
        compiler_params=_params(("parallel", "parallel", "arbitrary"), 56),
        name="flash_attn",
    )(q, k, v)


def _ln_silu(y, g, b):
    mu = jnp.mean(y, axis=-1, keepdims=True)
    yc = y - mu
    n = yc * lax.rsqrt(jnp.mean(yc * yc, axis=-1, keepdims=True) + EPS)
    return _silu(n * g + b)


def _merge_out(x, gsig, y_mla, y_conv, y_mem, w_out):
    d = x.shape[-1]
    m = gsig[:, :d] * y_mla + gsig[:, d:2 * d] * y_conv + gsig[:, 2 * d:] * y_mem
    return x + _dot(m.astype(BF16), w_out)


def _mixers_kernel(x_ref, o_ref, mk_ref, mv_ref, ga_ref, wb_ref, wdw_ref, bdw_ref, gln_ref, bln_ref, wco_ref,
                   gqm_ref, wom_ref, womla_ref, wout_ref, x1_ref, conv_ref, ubuf):
    j = pl.program_id(1)
    tm = x_ref.shape[0]
    x = x_ref[...]
    h = _rms(x, ga_ref[...]).astype(BF16)
    pb = _dot(h, wb_ref[...])
    u = pb[:, :C_CONV] * _sigmoid(pb[:, C_CONV:2 * C_CONV])
    qm = pb[:, 2 * C_CONV:2 * C_CONV + MEM_W]
    gsig = _sigmoid(pb[:, 2 * C_CONV + MEM_W:])

    @pl.when(j == 0)
    def _():
        ubuf[0:CONV_TAIL, :] = jnp.zeros((CONV_TAIL, C_CONV), F32)

    ubuf[CONV_TAIL:CONV_TAIL + tm, :] = u
    y = jnp.zeros((tm, C_CONV), F32) + bdw_ref[...]
    off = CONV_TAIL - (CONV_W - 1)
    for tap in range(CONV_W):
        y = y + wdw_ref[tap:tap + 1, :] * ubuf[off + tap:off + tap + tm, :]
    tail = ubuf[tm:tm + CONV_TAIL, :]
    ubuf[0:CONV_TAIL, :] = tail

    @pl.when(j == pl.num_programs(1) - 1)
    def _():
        conv_ref[0] = tail[CONV_TAIL - (CONV_W - 1):, :]

    y_conv = _dot(_ln_silu(y, gln_ref[...], bln_ref[...]).astype(BF16), wco_ref[...])

    mk = mk_ref[0].astype(BF16)
    mv = mv_ref[0].astype(BF16)
    heads = []
    for hd in range(MEM_HEADS):
        sl = slice(hd * MEM_HD, (hd + 1) * MEM_HD)
        qh = _rms(qm[:, sl], gqm_ref[...]).astype(BF16)
        s = _dot_nt(qh, mk[:, sl]) * MEM_SCALE
        e = jnp.exp(s - jnp.max(s, axis=-1, keepdims=True))
        p = e / jnp.sum(e, axis=-1, keepdims=True)
        heads.append(_dot(p.astype(BF16), mv[:, sl]))
    y_mem = _dot(jnp.concatenate(heads, axis=-1).astype(BF16), wom_ref[...])

    y_mla = _dot(o_ref[...], womla_ref[...])
    x1_ref[...] = _merge_out(x, gsig, y_mla, y_conv, y_mem, wout_ref[...])


def _mixers(x2d, o_attn, mk, mv, batch, seq, w):
    t, d = x2d.shape
    tm = min(256, seq)
    ns = seq // tm
    nb = w["w_b"].shape[1]
    mtok = mk.shape[1]
    row = lambda b, j: (b * ns + j, 0)
    return pl.pallas_call(
        _mixers_kernel,
        grid=(batch, ns),
        in_specs=[pl.BlockSpec((tm, d), row), pl.BlockSpec((tm, MLA_HEADS * V_HEAD), row),
                  pl.BlockSpec((1, mtok, MEM_W), lambda b, j: (b, 0, 0)),
                  pl.BlockSpec((1, mtok, MEM_W), lambda b, j: (b, 0, 0)),
                  _const_spec((1, d)), _const_spec((d, nb)), _const_spec((CONV_W, C_CONV)),
                  _const_spec((1, C_CONV)), _const_spec((1, C_CONV)), _const_spec((1, C_CONV)),
                  _const_spec((C_CONV, d)), _const_spec((1, MEM_HD)), _const_spec((MEM_W, d)),
                  _const_spec((MLA_HEADS * V_HEAD, d)), _const_spec((d, d))],
        out_specs=[pl.BlockSpec((tm, d), row),
                   pl.BlockSpec((1, CONV_W - 1, C_CONV), lambda b, j: (b, 0, 0))],
        out_shape=[jax.ShapeDtypeStruct((t, d), F32), jax.ShapeDtypeStruct((batch, CONV_W - 1, C_CONV), F32)],
        scratch_shapes=[pltpu.VMEM((tm + CONV_TAIL, C_CONV), F32)],
        compiler_params=_params(("parallel", "arbitrary"), 56),
        name="token_mixers",
    )(x2d, o_attn, mk, mv, w["g_attn"], w["w_b"], w["w_dw"], w["b_dw"], w["g_conv_ln"], w["b_conv_ln"],
      w["w_conv_out"], w["g_q_mem"], w["w_o_mem"], w["w_o_mla"], w["w_out"])


def _ffn_prep_kernel(x1_ref, g_ref, wrh_ref, wrl_ref, br_ref, wsg_ref, wsu_ref, wsd_ref,
                     base_ref, h2u_ref, idx_ref, w_ref):
    x1 = x1_ref[...]
    d = x1.shape[-1]
    h2 = _rms(x1, g_ref[...])
    hh, hl = _split(h2)
    wrh = wrh_ref[...]
    logits = _dot_nt(wrh, hh) + _dot_nt(wrh, hl) + _dot_nt(wrl_ref[...], hh)
    scores = _sigmoid(logits)
    val = scores + br_ref[...]
    eio = lax.broadcasted_iota(I32, val.shape, 0).astype(F32)
    idxs, ws = [], []
    for _ in range(TOP_K):
        m = jnp.max(val, axis=0, keepdims=True)
        sel = jnp.min(jnp.where(val == m, eio, float(N_EXPERTS)), axis=0, keepdims=True)
        hit = eio == sel
        ws.append(jnp.sum(jnp.where(hit, scores, 0.0), axis=0, keepdims=True))
        idxs.append(sel)
        val = jnp.where(hit, -jnp.inf, val)
    wk = jnp.concatenate(ws, axis=0)
    idx_ref[...] = jnp.concatenate(idxs, axis=0).astype(I32)
    w_ref[...] = wk / jnp.sum(wk, axis=0, keepdims=True) * ROUTED_SCALE

    a = _silu(_dot(hh, wsg_ref[...])) * _dot(hh, wsu_ref[...])
    base_ref[...] = x1 + _dot(a.astype(BF16), wsd_ref[...])

    bits = lax.bitcast_convert_type(hh.astype(F32), U32)
    lo = lax.shift_right_logical(bits[:, :d // 2], jnp.uint32(16))
    hi = bits[:, d // 2:] & jnp.uint32(0xFFFF0000)
    h2u_ref[...] = hi | lo


def _ffn_prep(x1, w, tm):
    t, d = x1.shape
    ds = w["w_s_gate"].shape[1]
    row = lambda i: (i, 0)
    col = lambda i: (0, i)
    return pl.pallas_call(
        _ffn_prep_kernel,
        grid=(t // tm,),
        in_specs=[pl.BlockSpec((tm, d), row), _const_spec((1, d)), _const_spec((N_EXPERTS, d)),
                  _const_spec((N_EXPERTS, d)), _const_spec((N_EXPERTS, 1)), _const_spec((d, ds)),
                  _const_spec((d, ds)), _const_spec((ds, d))],
        out_specs=[pl.BlockSpec((tm, d), row), pl.BlockSpec((tm, d // 2), row),
                   pl.BlockSpec((TOP_K, tm), col), pl.BlockSpec((TOP_K, tm), col)],
        out_shape=[jax.ShapeDtypeStruct((t, d), F32), jax.ShapeDtypeStruct((t, d // 2), U32),
                   jax.ShapeDtypeStruct((TOP_K, t), I32), jax.ShapeDtypeStruct((TOP_K, t), F32)],
        compiler_params=_params(("parallel",), 40),
        name="ffn_prep",
    )(x1, w["g_ffn"], w["w_r_hi"], w["w_r_lo"], w["b_router"], w["w_s_gate"], w["w_s_up"], w["w_s_down"])


def _route_kernel(idx_ref, tri_ref, dest_ref, blk_ref, nvalid_ref, cnt_scr, run_scr, *, blk):
    p = pl.program_id(0)
    i = pl.program_id(1)
    tm = idx_ref.shape[1]
    idx = idx_ref[...]
    eio = lax.broadcasted_iota(I32, (N_EXPERTS, tm), 0)
    onehot = jnp.zeros((N_EXPERTS, tm), F32)
    for k in range(TOP_K):
        onehot = onehot + (eio == idx[k:k + 1, :]).astype(F32)

    @pl.when((p == 0) & (i == 0))
    def _():
        cnt_scr[...] = jnp.zeros(cnt_scr.shape, F32)

    @pl.when(p == 0)
    def _():
        cnt_scr[...] += jnp.sum(onehot, axis=1, keepdims=True)

    @pl.when((p == 1) & (i == 0))
    def _():
        cnt = cnt_scr[...]
        padded = jnp.ceil(cnt * (1.0 / blk)) * blk
        r = lax.broadcasted_iota(I32, (N_EXPERTS, N_EXPERTS), 0)
        c = lax.broadcasted_iota(I32, (N_EXPERTS, N_EXPERTS), 1)
        pb = jnp.broadcast_to(padded, (N_EXPERTS, N_EXPERTS))
        prow = jnp.sum(jnp.where(r == c, pb, 0.0), axis=0, keepdims=True)
        prb = jnp.broadcast_to(prow, (N_EXPERTS, N_EXPERTS))
        pstart = jnp.sum(jnp.where(c < r, prb, 0.0), axis=1, keepdims=True)
        run_scr[...] = pstart
        pend = pstart + padded
        nb = blk_ref.shape[1]
        bstart = (lax.broadcasted_iota(I32, (N_EXPERTS, nb), 1) * blk).astype(F32)
        pend_b = jnp.broadcast_to(pend, (N_EXPERTS, nb))
        pstart_b = jnp.broadcast_to(pstart, (N_EXPERTS, nb))
        be = jnp.sum((pend_b <= bstart).astype(F32), axis=0, keepdims=True)
        blk_ref[...] = jnp.minimum(be, N_EXPERTS - 1.0).astype(I32)
        inside = (pstart_b <= bstart) & (bstart < pend_b)
        rows = jnp.clip(jnp.broadcast_to(cnt, (N_EXPERTS, nb)) - (bstart - pstart_b), 0.0, float(blk))
        nvalid_ref[...] = jnp.sum(jnp.where(inside, rows, 0.0), axis=0, keepdims=True).astype(I32)

    @pl.when(p == 1)
    def _():
        within = _dot(onehot.astype(BF16), tri_ref[...])
        slot = run_scr[...] + within
        rows = []
        for k in range(TOP_K):
            rows.append(jnp.sum(jnp.where(eio == idx[k:k + 1, :], slot, 0.0), axis=0, keepdims=True))
        dest_ref[...] = jnp.concatenate(rows, axis=0).astype(I32)
        run_scr[...] += jnp.sum(onehot, axis=1, keepdims=True)


def _route(idx_pad, blk, nb_pad):
    tpad = idx_pad.shape[1]
    tm = 512
    tri = (lax.broadcasted_iota(I32, (tm, tm), 0) < lax.broadcasted_iota(I32, (tm, tm), 1)).astype(BF16)
    return pl.pallas_call(
        functools.partial(_route_kernel, blk=blk),
        grid=(2, tpad // tm),
        in_specs=[pl.BlockSpec((TOP_K, tm), lambda p, i: (0, i)), _const_spec((tm, tm))],
        out_specs=[pl.BlockSpec((TOP_K, tm), lambda p, i: (0, i * p)), _const_spec((1, nb_pad)),
                   _const_spec((1, nb_pad))],
        out_shape=[jax.ShapeDtypeStruct((TOP_K, tpad), I32), jax.ShapeDtypeStruct((1, nb_pad), I32),
                   jax.ShapeDtypeStruct((1, nb_pad), I32)],
        scratch_shapes=[pltpu.VMEM((N_EXPERTS, 1), F32), pltpu.VMEM((N_EXPERTS, 1), F32)],
        compiler_params=_params(("arbitrary", "arbitrary"), 32),
        name="moe_route",
    )(idx_pad, tri)


def _dispatch_kernel(dest_ref, h2u_ref, xs_in_ref, xs_ref, sem):
    del xs_in_ref
    tm = dest_ref.shape[1]

    def row_copy(t, k):
        return pltpu.make_async_copy(h2u_ref.at[pl.ds(t, 1)], xs_ref.at[pl.ds(dest_ref[k, t], 1)], sem)

    def body(t, carry):
        for k in range(TOP_K):
            row_copy(t, k).start()
        return carry

    lax.fori_loop(0, tm, body, 0)

    def drain(t, carry):
        for k in range(TOP_K):
            row_copy(t, k).wait()
        return carry

    lax.fori_loop(0, tm, drain, 0)


def _dispatch(dest, h2u, xs, tm):
    t, half = h2u.shape
    return pl.pallas_call(
        _dispatch_kernel,
        grid=(t // tm,),
        in_specs=[pl.BlockSpec((TOP_K, tm), lambda i: (0, i), memory_space=pltpu.SMEM),
                  pl.BlockSpec((tm, half), lambda i: (i, 0)), pl.BlockSpec(memory_space=pl.ANY)],
        out_specs=pl.BlockSpec(memory_space=pl.ANY),
        out_shape=jax.ShapeDtypeStruct(xs.shape, xs.dtype),
        scratch_shapes=[pltpu.SemaphoreType.DMA(())],
        input_output_aliases={2: 0},
        compiler_params=pltpu.CompilerParams(dimension_semantics=("arbitrary",), has_side_effects=True),
        name="moe_dispatch",
    )(dest, h2u, xs)


def _expert_kernel(be_ref, nv_ref, xs_ref, wg_ref, wu_ref, wd_ref, y_ref):
    b = pl.program_id(0)
    half = xs_ref.shape[1]

    @pl.when(nv_ref[b] > 0)
    def _():
        word = xs_ref[...]
        lo = lax.bitcast_convert_type(lax.shift_left(word, jnp.uint32(16)), F32).astype(BF16)
        hi = lax.bitcast_convert_type(word & jnp.uint32(0xFFFF0000), F32).astype(BF16)
        g = _dot(lo, wg_ref[0, :half, :]) + _dot(hi, wg_ref[0, half:, :])
        u = _dot(lo, wu_ref[0, :half, :]) + _dot(hi, wu_ref[0, half:, :])
        y_ref[...] = _dot((_silu(g) * u).astype(BF16), wd_ref[0])

    @pl.when(nv_ref[b] == 0)
    def _():
        y_ref[...] = jnp.zeros(y_ref.shape, F32)


def _experts(block_e, nvalid, xs, wg, wu, wd, blk):
    nslots, half = xs.shape
    d = 2 * half
    de = wg.shape[2]
    grid_spec = pltpu.PrefetchScalarGridSpec(
        num_scalar_prefetch=2,
        grid=(nslots // blk,),
        in_specs=[pl.BlockSpec((blk, half), lambda b, be, nu: (b, 0)),
                  pl.BlockSpec((1, d, de), lambda b, be, nu: (be[b], 0, 0)),
                  pl.BlockSpec((1, d, de), lambda b, be, nu: (be[b], 0, 0)),
                  pl.BlockSpec((1, de, d), lambda b, be, nu: (be[b], 0, 0))],
        out_specs=pl.BlockSpec((blk, d), lambda b, be, nu: (b, 0)),
    )
    return pl.pallas_call(
        _expert_kernel,
        grid_spec=grid_spec,
        out_shape=jax.ShapeDtypeStruct((nslots, d), F32),
        compiler_params=_params(("arbitrary",), 32),
        name="moe_experts",
    )(block_e, nvalid, xs, wg, wu, wd)


def _combine_kernel(dcur_ref, dnext_ref, w_ref, base_ref, y_ref, out_ref, buf, sem):
    i = pl.program_id(0)
    n = pl.num_programs(0)
    tm = base_ref.shape[0]

    def copies(dref, slot, t, k):
        return pltpu.make_async_copy(y_ref.at[pl.ds(dref[k, t], 1)], buf.at[slot, pl.ds(k * tm + t, 1)],
                                     sem.at[slot])

    def issue(dref, slot):
        def body(t, carry):
            for k in range(TOP_K):
                copies(dref, slot, t, k).start()
            return carry
        lax.fori_loop(0, tm, body, 0)

    @pl.when(i == 0)
    def _():
        issue(dcur_ref, 0)

    @pl.when(i + 1 < n)
    def _():
        issue(dnext_ref, (i + 1) % 2)

    slot = i % 2

    def drain(t, carry):
        for k in range(TOP_K):
            copies(dcur_ref, slot, t, k).wait()
        return carry

    lax.fori_loop(0, tm, drain, 0)

    acc = base_ref[...]
    w = w_ref[...]
    for k in range(TOP_K):
        acc = acc + w[:, k:k + 1] * buf[slot, k * tm:(k + 1) * tm, :]
    out_ref[...] = acc


def _combine(dest, w_t, base, y, tm):
    t, d = base.shape
    n = t // tm
    return pl.pallas_call(
        _combine_kernel,
        grid=(n,),
        in_specs=[pl.BlockSpec((TOP_K, tm), lambda i: (0, i), memory_space=pltpu.SMEM),
                  pl.BlockSpec((TOP_K, tm), lambda i: (0, jnp.minimum(i + 1, n - 1)), memory_space=pltpu.SMEM),
                  pl.BlockSpec((tm, TOP_K), lambda i: (i, 0)), pl.BlockSpec((tm, d), lambda i: (i, 0)),
                  pl.BlockSpec(memory_space=pl.ANY)],
        out_specs=pl.BlockSpec((tm, d), lambda i: (i, 0)),
        out_shape=jax.ShapeDtypeStruct((t, d), F32),
        scratch_shapes=[pltpu.VMEM((2, TOP_K * tm, d), F32), pltpu.SemaphoreType.DMA((2,))],
        compiler_params=_params(("arbitrary",), 32),
        name="moe_combine",
    )(dest, dest, w_t, base, y)


def _sample_in_kernel(x_ref, ga_ref, wa_ref, gq_ref, wuq_ref, gqm_ref, gkv_ref, wuk_ref, e32_ref, gkm_ref,
                      gkn_ref, cos_ref, sa_ref, sb_ref, wb_ref, st_ref, wdw_ref, bdw_ref, gln_ref, bln_ref,
                      wco_ref, gqmem_ref,
                      c_ref, kr_ref, u_ref, qmn_ref, gsig_ref, yconv_ref, qabs_ref, qrope_ref, snew_ref):
    h, q_raw, k_raw, c, _, kr = _qkv_core(x_ref[...], ga_ref[...], wa_ref[...], gq_ref[...], wuq_ref[...],
                                          gqm_ref[...], gkv_ref[...], wuk_ref[...], e32_ref[...], gkm_ref[...])
    c_ref[...] = c
    kr_ref[...] = kr
    cos, sa, sb = cos_ref[...], sa_ref[...], sb_ref[...]
    for hd in range(MLA_HEADS):
        sl = slice(hd * HEAD_PAD, (hd + 1) * HEAD_PAD)
        qh = _rope128(_head_norm(q_raw[:, sl], gqm_ref[:, sl]), cos, sa, sb)
        kh = _rope128(_head_norm(k_raw[:, sl], gkm_ref[:, sl]), cos, sa, sb)
        snew_ref[:, hd:hd + 1] = jnp.sum(qh * kh, axis=-1, keepdims=True) * MLA_SCALE
        qrope_ref[hd] = qh[:, QK_NOPE:QK_HEAD]
        g_hi, g_lo = _split(qh * gkn_ref[:, sl])
        wk = wuk_ref[:, sl]
        qabs_ref[hd] = _dot_nt(g_hi, wk) + _dot_nt(g_lo, wk)

    pb = _dot(h, wb_ref[...])
    u = pb[:, :C_CONV] * _sigmoid(pb[:, C_CONV:2 * C_CONV])
    u_ref[...] = u
    for hd in range(MEM_HEADS):
        sl = slice(2 * C_CONV + hd * MEM_HD, 2 * C_CONV + (hd + 1) * MEM_HD)
        qmn_ref[:, hd * MEM_HD:(hd + 1) * MEM_HD] = _rms(pb[:, sl], gqmem_ref[...])
    gsig_ref[...] = _sigmoid(pb[:, 2 * C_CONV + MEM_W:])

    y = bdw_ref[...] + wdw_ref[CONV_W - 1:CONV_W, :] * u
    for tap in range(CONV_W - 1):
        y = y + wdw_ref[tap:tap + 1, :] * st_ref[tap]
    yconv_ref[...] = _dot(_ln_silu(y, gln_ref[...], bln_ref[...]).astype(BF16), wco_ref[...])


def _sample_in(xs2d, state_t, w, tabs):
    db, d = xs2d.shape
    ins = [xs2d, w["g_attn"], w["w_a"], w["g_q_lora"], w["w_uq_p"], w["g_q_mla_p"], w["g_kv_lora"], w["w_uk_p"],
           w["e32"], w["g_k_mla_p"], w["g_k_nope_p"], *tabs, w["w_b"], state_t, w["w_dw"], w["b_dw"],
           w["g_conv_ln"], w["b_conv_ln"], w["w_conv_out"], w["g_q_mem"]]
    outs = [jax.ShapeDtypeStruct((db, KV_LORA), F32), jax.ShapeDtypeStruct((db, QK_ROPE), F32),
            jax.ShapeDtypeStruct((db, C_CONV), F32), jax.ShapeDtypeStruct((db, MEM_W), F32),
            jax.ShapeDtypeStruct((db, N_BRANCH * d), F32), jax.ShapeDtypeStruct((db, d), F32),
            jax.ShapeDtypeStruct((MLA_HEADS, db, KV_LORA), F32), jax.ShapeDtypeStruct((MLA_HEADS, db, QK_ROPE), F32),
            jax.ShapeDtypeStruct((db, MLA_HEADS), F32)]
    return pl.pallas_call(
        _sample_in_kernel,
        grid=(1,),
        in_specs=[_const_spec(a.shape) for a in ins],
        out_specs=[_const_spec(o.shape) for o in outs],
        out_shape=outs,
        compiler_params=_params(("arbitrary",), 56),
        name="sample_in",
    )(*ins)


def _sample_mla_kernel(pt_ref, qabs_ref, qrope_ref, wukt_ref, g1_ref, g2_ref, cost_ref, sint_ref,
                       lat_ref, kro_ref, ctx_ref, m_ref, l_ref, cbuf, krbuf, lhs, sem_c, sem_k, *, n_chunks, ch):
    b = pl.program_id(0)
    nseq = pl.num_programs(0)
    page = cbuf.shape[2]
    pc = ch * page
    nrow = MLA_HEADS * QK_NOPE

    def page_copies(seq, chunk, slot, p):
        pid = pt_ref[seq, chunk * ch + p]
        return (pltpu.make_async_copy(lat_ref.at[pid], cbuf.at[slot, p], sem_c.at[slot]),
                pltpu.make_async_copy(kro_ref.at[pid], krbuf.at[slot, p], sem_k.at[slot]))

    def issue(seq, chunk, slot):
        for p in range(ch):
            a, k = page_copies(seq, chunk, slot, p)
            a.start()
            k.start()

    @pl.when(b == 0)
    def _():
        lhs[0:nrow, :] = wukt_ref[...]
        issue(0, 0, 0)

    qa = jnp.concatenate([qabs_ref[0], jnp.zeros((16 - MLA_HEADS, KV_LORA), F32)], axis=0)
    lhs[nrow:nrow + 16, :] = qa.astype(BF16)
    qr = qrope_ref[0].astype(BF16)
    g1 = g1_ref[...]
    g2 = g2_ref[...]

    def chunk_body(c, carry):
        m_prev, l_prev, ctx = carry
        slot = c % 2

        @pl.when(c + 1 < n_chunks)
        def _():
            issue(b, c + 1, 1 - slot)

        @pl.when((c + 1 == n_chunks) & (b + 1 < nseq))
        def _():
            issue(b + 1, 0, 1 - slot)

        for p in range(ch):
            a, k = page_copies(b, c, slot, p)
            a.wait()
            k.wait()

        cb = cbuf[slot].reshape(pc, KV_LORA).astype(BF16)
        kt = _dot_nt(lhs[...], cb)
        ssn = jnp.concatenate(
            [jnp.sum(jnp.square(kt[h * QK_NOPE:(h + 1) * QK_NOPE]), axis=0, keepdims=True)
             for h in range(MLA_HEADS)], axis=0)
        s_nope = kt[nrow:nrow + MLA_HEADS]

        krt = jnp.concatenate([krbuf[slot, p] for p in range(ch)], axis=1)
        ssr = jnp.sum(krt * krt, axis=0, keepdims=True)
        half = QK_ROPE // 2
        rot = jnp.concatenate([-krt[half:], krt[:half]], axis=0)
        roped = krt * g1 * cost_ref[c] + rot * g2 * sint_ref[c]
        s_rope = _dot(qr, roped.astype(BF16))

        rinv = lax.rsqrt((ssn + ssr) * (1.0 / QK_HEAD) + EPS)
        s = (s_nope + s_rope) * rinv * MLA_SCALE
        m_new = jnp.maximum(m_prev, jnp.max(s, axis=-1, keepdims=True))
        alpha = jnp.exp(m_prev - m_new)
        pexp = jnp.exp(s - m_new)
        l_new = alpha * l_prev + jnp.sum(pexp, axis=-1, keepdims=True)
        ctx_new = alpha * ctx + _dot(pexp.astype(BF16), cb)
        return m_new, l_new, ctx_new

    init = (jnp.full((MLA_HEADS, 1), NEG_BIG, F32), jnp.zeros((MLA_HEADS, 1), F32),
            jnp.zeros((MLA_HEADS, KV_LORA), F32))
    m_fin, l_fin, ctx = lax.fori_loop(0, n_chunks, chunk_body, init)
    ctx_ref[0] = ctx
    m_ref[0] = jnp.broadcast_to(m_fin, (MLA_HEADS, LANES))
    l_ref[0] = jnp.broadcast_to(l_fin, (MLA_HEADS, LANES))


def _sample_mla(page_table, qabs, qrope, lat, kro, w, ch):
    db, n_pages = page_table.shape
    page = lat.shape[1]
    n_chunks = n_pages // ch
    pc = ch * page
    nrow = MLA_HEADS * QK_NOPE
    grid_spec = pltpu.PrefetchScalarGridSpec(
        num_scalar_prefetch=1,
        grid=(db,),
        in_specs=[pl.BlockSpec((1, MLA_HEADS, KV_LORA), lambda b, pt: (b, 0, 0)),
                  pl.BlockSpec((1, MLA_HEADS, QK_ROPE), lambda b, pt: (b, 0, 0)),
                  pl.BlockSpec((nrow, KV_LORA), lambda b, pt: (0, 0)),
                  pl.BlockSpec((QK_ROPE, 1), lambda b, pt: (0, 0)),
                  pl.BlockSpec((QK_ROPE, 1), lambda b, pt: (0, 0)),
                  pl.BlockSpec((n_chunks, QK_ROPE, pc), lambda b, pt: (0, 0, 0)),
                  pl.BlockSpec((n_chunks, QK_ROPE, pc), lambda b, pt: (0, 0, 0)),
                  pl.BlockSpec(memory_space=pl.ANY), pl.BlockSpec(memory_space=pl.ANY)],
        out_specs=[pl.BlockSpec((1, MLA_HEADS, KV_LORA), lambda b, pt: (b, 0, 0)),
                   pl.BlockSpec((1, MLA_HEADS, LANES), lambda b, pt: (b, 0, 0)),
                   pl.BlockSpec((1, MLA_HEADS, LANES), lambda b, pt: (b, 0, 0))],
        scratch_shapes=[pltpu.VMEM((2, ch, page, KV_LORA), F32), pltpu.VMEM((2, ch, QK_ROPE, page), F32),
                        pltpu.VMEM((nrow + 16, KV_LORA), BF16),
                        pltpu.SemaphoreType.DMA((2,)), pltpu.SemaphoreType.DMA((2,))],
    )
    return pl.pallas_call(
        functools.partial(_sample_mla_kernel, n_chunks=n_chunks, ch=ch),
        grid_spec=grid_spec,
        out_shape=[jax.ShapeDtypeStruct((db, MLA_HEADS, KV_LORA), F32),
                   jax.ShapeDtypeStruct((db, MLA_HEADS, LANES), F32),
                   jax.ShapeDtypeStruct((db, MLA_HEADS, LANES), F32)],
        compiler_params=_params(("arbitrary",), 56),
        name="sample_mla",
    )(page_table, qabs, qrope, w["w_uk_t"], w["g_kr1"], w["g_kr2"], w["cos_t"], w["sin_t"],
      lat, kro)


def _sample_mem_kernel(q_ref, k_ref, v_ref, o_ref):
    q = q_ref[0]
    s = jnp.sum(k_ref[0] * q, axis=-1, keepdims=True) * MEM_SCALE
    e = jnp.exp(s - jnp.max(s, axis=0, keepdims=True))
    p = e / jnp.sum(e, axis=0, keepdims=True)
    o_ref[0] = jnp.sum(p * v_ref[0], axis=0)


def _sample_mem(qmn, mk, mv):
    db, mtok = mk.shape[:2]
    q3 = qmn.reshape(db, MEM_HEADS, MEM_HD)
    kv_spec = pl.BlockSpec((1, mtok, MEM_HEADS, MEM_HD), lambda b: (b, 0, 0, 0))
    q_spec = pl.BlockSpec((1, MEM_HEADS, MEM_HD), lambda b: (b, 0, 0))
    return pl.pallas_call(
        _sample_mem_kernel,
        grid=(db,),
        in_specs=[q_spec, kv_spec, kv_spec],
        out_specs=q_spec,
        out_shape=jax.ShapeDtypeStruct((db, MEM_HEADS, MEM_HD), F32),
        compiler_params=_params(("parallel",), 32),
        name="sample_mem",
    )(q3, mk, mv).reshape(db, MEM_W)


def _sample_out_kernel(x_ref, ctx_ref, m_ref, l_ref, snew_ref, c_ref, wuv_ref, womla_ref, omem_ref, wom_ref,
                       gsig_ref, yconv_ref, wout_ref, x1_ref):
    c_new = c_ref[...]
    m = m_ref[...]
    l = l_ref[...]
    s_new = snew_ref[...]
    d = x_ref.shape[-1]
    y_mla = jnp.zeros((x_ref.shape[0], d), F32)
    for h in range(MLA_HEADS):
        mh, lh, sh = m[:, h:h + 1], l[:, h:h + 1], s_new[:, h:h + 1]
        m_fin = jnp.maximum(mh, sh)
        a_old = jnp.exp(mh - m_fin)
        a_new = jnp.exp(sh - m_fin)
        denom = lh * a_old + a_new
        ctx = (ctx_ref[h] * a_old + a_new * c_new) / denom
        o_h = _dot(ctx.astype(BF16), wuv_ref[h])
        y_mla = y_mla + _dot(o_h.astype(BF16), womla_ref[h])
    y_mem = _dot(omem_ref[...].astype(BF16), wom_ref[...])
    x1_ref[...] = _merge_out(x_ref[...], gsig_ref[...], y_mla, yconv_ref[...], y_mem, wout_ref[...])


def _sample_out(xs2d, ctx_t, m, l, s_new, c_s, o_mem, gsig, y_conv, w):
    ins = [xs2d, ctx_t, m, l, s_new, c_s, w["w_uv_h"], w["w_o_mla_h"], o_mem, w["w_o_mem"], gsig, y_conv,
           w["w_out"]]
    return pl.pallas_call(
        _sample_out_kernel,
        grid=(1,),
        in_specs=[_const_spec(a.shape) for a in ins],
        out_specs=_const_spec(xs2d.shape),
        out_shape=jax.ShapeDtypeStruct(xs2d.shape, F32),
        compiler_params=_params(("arbitrary",), 40),
        name="sample_out",
    )(*ins)


def _prep_weights(g_attn_norm, w_in, g_q_lora, w_uq, g_q_mla, g_kv_lora, w_uk, w_uv, g_k_mla, w_o_mla, w_dw, b_dw,
                  g_conv_ln, b_conv_ln, w_conv_out, g_mem_norm, w_mem_kv, g_q_mem, g_k_mem, w_o_mem, w_out,
                  g_ffn_norm, w_router, b_router, w_e_gate, w_e_up, w_e_down, w_s_gate, w_s_up, w_s_down):
    row = lambda g: g.reshape(1, -1).astype(F32)
    e32 = jnp.zeros((QK_ROPE, MLA_HEADS, HEAD_PAD), F32)
    e32 = e32.at[jnp.arange(QK_ROPE), :, QK_NOPE + jnp.arange(QK_ROPE)].set(1.0)
    half = QK_ROPE // 2
    g_rope = g_k_mla[QK_NOPE:]
    w_r_hi = w_router.T.astype(BF16)
    w_r_lo = (w_router.T - w_r_hi.astype(F32)).astype(BF16)
    g_k_nope = jnp.concatenate([g_k_mla[:QK_NOPE], jnp.zeros((QK_ROPE,), F32)])
    return {
        "g_attn": row(g_attn_norm),
        "w_a": w_in[:, :QKV_COLS].astype(BF16),
        "w_b": w_in[:, QKV_COLS:].astype(BF16),
        "g_q_lora": row(g_q_lora),
        "w_uq_p": _pad_heads(w_uq, QK_HEAD).astype(BF16),
        "g_q_mla_p": _pad_heads(jnp.tile(g_q_mla, MLA_HEADS).reshape(1, -1), QK_HEAD),
        "g_kv_lora": row(g_kv_lora),
        "w_uk_p": _pad_heads(w_uk.reshape(KV_LORA, MLA_HEADS * QK_NOPE), QK_NOPE).astype(BF16),
        "e32": e32.reshape(QK_ROPE, MLA_HEADS * HEAD_PAD).astype(BF16),
        "g_k_mla_p": _pad_heads(jnp.tile(g_k_mla, MLA_HEADS).reshape(1, -1), QK_HEAD),
        "g_k_nope_p": _pad_heads(jnp.tile(g_k_nope, MLA_HEADS).reshape(1, -1), QK_HEAD),
        "w_uv_p": _pad_heads(w_uv.reshape(KV_LORA, MLA_HEADS * V_HEAD), V_HEAD).astype(BF16),
        "w_uv_h": w_uv.transpose(1, 0, 2).astype(BF16),
        "w_uk_t": w_uk.reshape(KV_LORA, MLA_HEADS * QK_NOPE).T.astype(BF16),
        "g_kr1": g_rope.reshape(QK_ROPE, 1),
        "g_kr2": jnp.concatenate([g_rope[half:], g_rope[:half]]).reshape(QK_ROPE, 1),
        "w_o_mla": w_o_mla.astype(BF16),
        "w_o_mla_h": w_o_mla.reshape(MLA_HEADS, V_HEAD, -1).astype(BF16),
        "w_dw": w_dw.astype(F32), "b_dw": row(b_dw), "g_conv_ln": row(g_conv_ln), "b_conv_ln": row(b_conv_ln),
        "w_conv_out": w_conv_out.astype(BF16),
        "g_mem_norm": row(g_mem_norm), "w_mem_kv": w_mem_kv.astype(BF16),
        "g_q_mem": row(g_q_mem), "g_k_mem": row(g_k_mem),
        "w_o_mem": w_o_mem.astype(BF16), "w_out": w_out.astype(BF16),
        "g_ffn": row(g_ffn_norm), "w_r_hi": w_r_hi, "w_r_lo": w_r_lo,
        "b_router": b_router.reshape(N_EXPERTS, 1).astype(F32),
        "w_e_gate": w_e_gate.astype(BF16), "w_e_up": w_e_up.astype(BF16), "w_e_down": w_e_down.astype(BF16),
        "w_s_gate": w_s_gate.astype(BF16), "w_s_up": w_s_up.astype(BF16), "w_s_down": w_s_down.astype(BF16),
    }


def _layer(xp, xs, mem_prompt, cache_latent, cache_krope, state_conv, cache_mem_k, cache_mem_v, page_table, w):
    b, s, d = xp.shape
    db, ds, _ = xs.shape
    assert ds == 1, "the sample group decodes one token per sequence"
    n_pages, page = page_table.shape[1], cache_latent.shape[1]
    past = n_pages * page
    mtok = mem_prompt.shape[1]
    xp2 = xp.reshape(b * s, d)
    xs2 = xs.reshape(db * ds, d)

    mk_p, mv_p = _mem_kv(mem_prompt.reshape(b * mtok, d), w["g_mem_norm"], w["w_mem_kv"], w["g_k_mem"])
    q, k, v, c_p, kr_p = _qkv(xp2, s, w, _rope_tables(jnp.arange(s)))
    o_attn = _flash(q, k, v, b, s)
    x1_p, conv_p = _mixers(xp2, o_attn, mk_p.reshape(b, mtok, MEM_W), mv_p.reshape(b, mtok, MEM_W), b, s, w)

    ch = math.gcd(n_pages, 16)
    pc = ch * page
    half = QK_ROPE // 2
    inv = ROPE_THETA ** (-jnp.arange(half, dtype=F32) / half)
    ang = inv[:, None] * jnp.arange(past, dtype=F32)[None, :]
    tab = lambda a: jnp.concatenate([a, a], axis=0).reshape(QK_ROPE, past // pc, pc).transpose(1, 0, 2)
    ws = dict(w, cos_t=tab(jnp.cos(ang)), sin_t=tab(jnp.sin(ang)))
    tabs_s = _rope_tables(jnp.full((db,), past, I32))
    (c_s, kr_s, u_s, qmn, gsig_s, yconv_s, qabs, qrope, s_new) = _sample_in(
        xs2, state_conv.transpose(1, 0, 2), w, tabs_s)
    ctx, m_s, l_s = _sample_mla(page_table, qabs.transpose(1, 0, 2), qrope.transpose(1, 0, 2),
                                cache_latent, cache_krope.transpose(0, 2, 1), ws, ch)
    o_mem = _sample_mem(qmn, cache_mem_k, cache_mem_v)
    x1_s = _sample_out(xs2, ctx.transpose(1, 0, 2), m_s[:, :, 0], l_s[:, :, 0], s_new, c_s, o_mem, gsig_s,
                       yconv_s, w)

    base_p, h2u_p, idx_p, wk_p = _ffn_prep(x1_p, w, min(512, b * s))
    base_s, h2u_s, idx_s, wk_s = _ffn_prep(x1_s, w, db)
    tp, tsm = b * s, db
    t_all = tp + tsm
    blk = 512
    route_tile = 512
    t_pad = -(-t_all // route_tile) * route_tile
    idx_all = jnp.concatenate([idx_p, idx_s, jnp.full((TOP_K, t_pad - t_all), N_EXPERTS, I32)], axis=1)
    n_blocks = -(-t_all * TOP_K // blk) + N_EXPERTS
    nb_pad = -(-n_blocks // LANES) * LANES
    dest, block_e, nvalid = _route(idx_all, blk, nb_pad)
    xs_sorted = jnp.zeros((n_blocks * blk, d // 2), U32)
    xs_sorted = _dispatch(dest[:, :tp], h2u_p, xs_sorted, min(256, tp))
    xs_sorted = _dispatch(dest[:, tp:t_all], h2u_s, xs_sorted, tsm)
    y = _experts(block_e[0, :n_blocks], nvalid[0, :n_blocks], xs_sorted, w["w_e_gate"], w["w_e_up"], w["w_e_down"], blk)
    yp = _combine(dest[:, :tp], wk_p.T, base_p, y, 128)
    ys = _combine(dest[:, tp:t_all], wk_s.T, base_s, y, min(128, tsm))

    new_conv_s = jnp.concatenate([state_conv[:, 1:], u_s[:, None, :]], axis=1)
    return (yp.reshape(b, s, d), ys.reshape(db, ds, d), c_p.reshape(b, s, KV_LORA), kr_p.reshape(b, s, QK_ROPE),
            conv_p, mk_p.reshape(b, mtok, MEM_HEADS, MEM_HD), mv_p.reshape(b, mtok, MEM_HEADS, MEM_HD),
            c_s.reshape(db, ds, KV_LORA), kr_s.reshape(db, ds, QK_ROPE), new_conv_s)


def kernel(x_prompt, x_sample, mem_prompt, cache_latent, cache_krope, state_conv, cache_mem_k, cache_mem_v,
           page_table, g_attn_norm, w_in, g_q_lora, w_uq, g_q_mla, g_kv_lora, w_uk, w_uv, g_k_mla, w_o_mla, w_dw,
           b_dw, g_conv_ln, b_conv_ln, w_conv_out, g_mem_norm, w_mem_kv, g_q_mem, g_k_mem, w_o_mem, w_out,
           g_ffn_norm, w_router, b_router, w_e_gate, w_e_up, w_e_down, w_s_gate, w_s_up, w_s_down):
    depth = w_in.shape[0]
    params = (g_attn_norm, w_in, g_q_lora, w_uq, g_q_mla, g_kv_lora, w_uk, w_uv, g_k_mla, w_o_mla, w_dw, b_dw,
              g_conv_ln, b_conv_ln, w_conv_out, g_mem_norm, w_mem_kv, g_q_mem, g_k_mem, w_o_mem, w_out,
              g_ffn_norm, w_router, b_router, w_e_gate, w_e_up, w_e_down, w_s_gate, w_s_up, w_s_down)
    xp, xs = x_prompt, x_sample
    per_layer = []
    for layer in range(depth):
        w = _prep_weights(*[p[layer] for p in params])
        outs = _layer(xp, xs, mem_prompt, cache_latent[layer], cache_krope[layer], state_conv[layer],
                      cache_mem_k[layer], cache_mem_v[layer], page_table, w)
        xp, xs = outs[0], outs[1]
        per_layer.append(outs[2:])
    stacked = tuple(jnp.stack([pl_[i] for pl_ in per_layer]) for i in range(8))
    return (xp, xs) + stacked
```

```python
import functools
import math

import jax
import jax.numpy as jnp
from jax import lax
from jax.experimental import pallas as pl
from jax.experimental.pallas import tpu as pltpu

F32 = jnp.float32
BF16 = jnp.bfloat16
I32 = jnp.int32
U32 = jnp.uint32

MLA_HEADS = 8
QK_NOPE = 64
QK_ROPE = 32
QK_HEAD = QK_NOPE + QK_ROPE
V_HEAD = 64
Q_LORA = 256
KV_LORA = 256
ROPE_THETA = 10000.0
MLA_SCALE = QK_HEAD ** -0.5
C_CONV = 512
CONV_W = 31
MEM_HEADS = 4
MEM_HD = 128
MEM_W = MEM_HEADS * MEM_HD
MEM_SCALE = MEM_HD ** -0.5
N_BRANCH = 3
N_EXPERTS = 64
TOP_K = 8
D_EXPERT = 256
ROUTED_SCALE = 2.5
EPS = 1e-6
NEG_BIG = -1e30
Q_PRESCALE = MLA_SCALE * math.log2(math.e)

LANES = 128
HEAD_PAD = LANES
QKV_COLS = Q_LORA + KV_LORA + QK_ROPE
CONV_TAIL = 32
SUBLANES = 8
RUN_ALIGN = SUBLANES
ROUTE_SUB = 128
ROUTE_TILE = 512
MIB = 1024 * 1024


def _dot(a, b):
    return jnp.dot(a, b, preferred_element_type=F32)


def _dot_nt(a, b):
    return lax.dot_general(a, b, (((1,), (1,)), ((), ())), preferred_element_type=F32)


def _split(x):
    hi = x.astype(BF16)
    lo = (x - hi.astype(F32)).astype(BF16)
    return hi, lo


def _rms(x, g):
    return x * lax.rsqrt(jnp.mean(x * x, axis=-1, keepdims=True) + EPS) * g


def _head_norm(x, g):
    ss = jnp.sum(x * x, axis=-1, keepdims=True)
    return x * lax.rsqrt(ss * (1.0 / QK_HEAD) + EPS) * g


def _rope128(n, cos, sa, sb):
    return n * cos + pltpu.roll(n, LANES - QK_ROPE // 2, 1) * sa + pltpu.roll(n, QK_ROPE // 2, 1) * sb


def _sigmoid(x):
    return jax.nn.sigmoid(x)


def _silu(x):
    return x * jax.nn.sigmoid(x)


def _const_spec(shape):
    nd = len(shape)
    return pl.BlockSpec(shape, lambda *_: (0,) * nd)


def _params(sem, vmem_mib):
    return pltpu.CompilerParams(dimension_semantics=sem, vmem_limit_bytes=vmem_mib * MIB)


def _pad_heads(w, used):
    lead = w.shape[:-1]
    w = w.reshape(lead + (MLA_HEADS, used))
    w = jnp.pad(w, [(0, 0)] * len(lead) + [(0, 0), (0, HEAD_PAD - used)])
    return w.reshape(lead + (MLA_HEADS * HEAD_PAD,))


def _rope_tables(pos):
    half = QK_ROPE // 2
    inv = ROPE_THETA ** (-jnp.arange(half, dtype=F32) / half)
    ang = pos.astype(F32)[:, None] * inv[None, :]
    cos, sin = jnp.cos(ang), jnp.sin(ang)
    n = pos.shape[0]
    ones = jnp.ones((n, QK_NOPE), F32)
    zeros = jnp.zeros((n, QK_NOPE), F32)
    tail1 = jnp.ones((n, HEAD_PAD - QK_HEAD), F32)
    tail0 = jnp.zeros((n, HEAD_PAD - QK_HEAD), F32)
    z16 = jnp.zeros((n, half), F32)
    cosb = jnp.concatenate([ones, cos, cos, tail1], axis=1)
    sa = jnp.concatenate([zeros, -sin, z16, tail0], axis=1)
    sb = jnp.concatenate([zeros, z16, sin, tail0], axis=1)
    return cosb, sa, sb


def _mem_kv_kernel(mem_ref, gn_ref, w_ref, gk_ref, k_ref, v_ref):
    m = _rms(mem_ref[...], gn_ref[...]).astype(BF16)
    kv = _dot(m, w_ref[...])
    for h in range(MEM_HEADS):
        sl = slice(h * MEM_HD, (h + 1) * MEM_HD)
        k_ref[:, sl] = _rms(kv[:, sl], gk_ref[...])
    v_ref[...] = kv[:, MEM_W:]


def _mem_kv(mem2d, g_norm, w_kv_bf, g_k):
    n, d = mem2d.shape
    tm = 256
    return pl.pallas_call(
        _mem_kv_kernel,
        grid=(n // tm,),
        in_specs=[pl.BlockSpec((tm, d), lambda i: (i, 0)), _const_spec((1, d)), _const_spec((d, 2 * MEM_W)),
                  _const_spec((1, MEM_HD))],
        out_specs=[pl.BlockSpec((tm, MEM_W), lambda i: (i, 0)), pl.BlockSpec((tm, MEM_W), lambda i: (i, 0))],
        out_shape=[jax.ShapeDtypeStruct((n, MEM_W), F32), jax.ShapeDtypeStruct((n, MEM_W), F32)],
        compiler_params=_params(("parallel",), 32),
        name="mem_kv",
    )(mem2d, g_norm, w_kv_bf, g_k)


def _qkv_core(x, ga, wa, gq, wuq, gqm, gkv, wuk, e32, gkm):
    h = _rms(x, ga).astype(BF16)
    pa = _dot(h, wa)
    q_lat = pa[:, :Q_LORA]
    kv_lat = pa[:, Q_LORA:Q_LORA + KV_LORA]
    kr = pa[:, Q_LORA + KV_LORA:QKV_COLS]
    c_q = _rms(q_lat, gq).astype(BF16)
    q_raw = _dot(c_q, wuq)
    c = _rms(kv_lat, gkv)
    cb = c.astype(BF16)
    kr_hi, kr_lo = _split(kr)
    k_raw = _dot(cb, wuk) + _dot(kr_hi, e32) + _dot(kr_lo, e32)
    return h, q_raw, k_raw, c, cb, kr


def _qkv_kernel(x_ref, ga_ref, wa_ref, gq_ref, wuq_ref, gqm_ref, gkv_ref, wuk_ref, e32_ref, gkm_ref, wuv_ref,
                cos_ref, sa_ref, sb_ref, q_ref, k_ref, v_ref, c_ref, kr_ref):
    _, q_raw, k_raw, c, cb, kr = _qkv_core(x_ref[...], ga_ref[...], wa_ref[...], gq_ref[...], wuq_ref[...],
                                           gqm_ref[...], gkv_ref[...], wuk_ref[...], e32_ref[...], gkm_ref[...])
    c_ref[...] = c
    kr_ref[...] = kr
    lane = lax.broadcasted_iota(I32, (1, MLA_HEADS * HEAD_PAD), 1) % HEAD_PAD
    v_ref[...] = (_dot(cb, wuv_ref[...]) + (lane == V_HEAD).astype(F32)).astype(BF16)
    cos, sa, sb = cos_ref[...], sa_ref[...], sb_ref[...]
    for h in range(MLA_HEADS):
        sl = slice(h * HEAD_PAD, (h + 1) * HEAD_PAD)
        qh = _rope128(_head_norm(q_raw[:, sl], gqm_ref[:, sl]), cos, sa, sb)
        q_ref[:, sl] = (qh * Q_PRESCALE).astype(BF16)
        k_ref[:, sl] = _rope128(_head_norm(k_raw[:, sl], gkm_ref[:, sl]), cos, sa, sb).astype(BF16)


def _qkv(x2d, seq, w, tabs):
    t, d = x2d.shape
    tm = min(512, seq)
    nseq = seq // tm
    hp = MLA_HEADS * HEAD_PAD
    row = lambda i: (i, 0)
    tab = lambda i: (i % nseq, 0)
    return pl.pallas_call(
        _qkv_kernel,
        grid=(t // tm,),
        in_specs=[pl.BlockSpec((tm, d), row), _const_spec((1, d)), _const_spec((d, QKV_COLS)),
                  _const_spec((1, Q_LORA)), _const_spec((Q_LORA, hp)), _const_spec((1, hp)),
                  _const_spec((1, KV_LORA)), _const_spec((KV_LORA, hp)), _const_spec((QK_ROPE, hp)),
                  _const_spec((1, hp)), _const_spec((KV_LORA, hp)),
                  pl.BlockSpec((tm, HEAD_PAD), tab), pl.BlockSpec((tm, HEAD_PAD), tab),
                  pl.BlockSpec((tm, HEAD_PAD), tab)],
        out_specs=[pl.BlockSpec((tm, hp), row), pl.BlockSpec((tm, hp), row), pl.BlockSpec((tm, hp), row),
                   pl.BlockSpec((tm, KV_LORA), row), pl.BlockSpec((tm, QK_ROPE), row)],
        out_shape=[jax.ShapeDtypeStruct((t, hp), BF16), jax.ShapeDtypeStruct((t, hp), BF16),
                   jax.ShapeDtypeStruct((t, hp), BF16), jax.ShapeDtypeStruct((t, KV_LORA), F32),
                   jax.ShapeDtypeStruct((t, QK_ROPE), F32)],
        compiler_params=_params(("parallel",), 48),
        name="qkv_proj",
    )(x2d, w["g_attn"], w["w_a"], w["g_q_lora"], w["w_uq_p"], w["g_q_mla_p"], w["g_kv_lora"], w["w_uk_p"],
      w["e32"], w["g_k_mla_p"], w["w_uv_p"], *tabs)


def _flash_kernel(q_ref, k_ref, v_ref, o_ref, m_scr, acc_scr):
    i = pl.program_id(1)
    j = pl.program_id(2)
    tq = q_ref.shape[0]
    tk = k_ref.shape[0]
    last = i // 2

    @pl.when(j == 0)
    def _():
        m_scr[...] = jnp.full(m_scr.shape, NEG_BIG, F32)
        acc_scr[...] = jnp.zeros(acc_scr.shape, F32)

    def step(nkeys, shift):
        if shift is not None:
            row = lax.broadcasted_iota(I32, (tq, nkeys), 0)
            col = lax.broadcasted_iota(I32, (tq, nkeys), 1)
            keep = col <= row + shift
        for h in range(MLA_HEADS):
            sl = slice(h * HEAD_PAD, (h + 1) * HEAD_PAD)
            s = _dot_nt(q_ref[:, sl], k_ref[0:nkeys, sl])
            if shift is not None:
                s = jnp.where(keep, s, NEG_BIG)
            m_prev = m_scr[h]
            m_new = jnp.maximum(m_prev, jnp.max(s, axis=-1, keepdims=True))
            alpha = jnp.exp2(m_prev - m_new)
            p = jnp.exp2(s - jnp.tile(m_new, (1, nkeys // LANES)))
            acc_scr[h] = alpha * acc_scr[h] + _dot(p.astype(BF16), v_ref[0:nkeys, sl])
            m_scr[h] = m_new

    @pl.when(j < last)
    def _():
        step(tk, None)

    @pl.when((j == last) & (i % 2 == 0))
    def _():
        step(tq, 0)

    @pl.when((j == last) & (i % 2 == 1))
    def _():
        step(tk, tq)

    @pl.when(j == last)
    def _():
        for h in range(MLA_HEADS):
            acc = acc_scr[h]
            o = acc / acc[:, V_HEAD:V_HEAD + 1]
            o_ref[:, h * V_HEAD:(h + 1) * V_HEAD] = o[:, :V_HEAD].astype(BF16)


def _flash(q, k, v, batch, seq):
    t, hp = q.shape
    tq = min(512, seq // 2)
    tk = 2 * tq
    nq, nk = seq // tq, seq // tk
    qmap = lambda b, i, j: (b * nq + i, 0)
    kmap = lambda b, i, j: (b * nk + jnp.minimum(j, i // 2), 0)
    return pl.pallas_call(
        _flash_kernel,
        grid=(batch, nq, nk),
        in_specs=[pl.BlockSpec((tq, hp), qmap), pl.BlockSpec((tk, hp), kmap), pl.BlockSpec((tk, hp), kmap)],
        out_specs=pl.BlockSpec((tq, MLA_HEADS * V_HEAD), qmap),
        out_shape=jax.ShapeDtypeStruct((t, MLA_HEADS * V_HEAD), BF16),
        scratch_shapes=[pltpu.VMEM((MLA_HEADS, tq, LANES), F32), pltpu.VMEM((MLA_HEADS, tq, HEAD_PAD), F32)],
        compiler_params=_params(("parallel", "parallel", "arbitrary"), 56),
        name="flash_attn",
    )(q, k, v)


def _ln_silu(y, g, b):
    mu = jnp.mean(y, axis=-1, keepdims=True)
    yc = y - mu
    n = yc * lax.rsqrt(jnp.mean(yc * yc, axis=-1, keepdims=True) + EPS)
    return _silu(n * g + b)


def _merge_out(x, gsig, y_mla, y_conv, y_mem, w_out):
    d = x.shape[-1]
    m = gsig[:, :d] * y_mla + gsig[:, d:2 * d] * y_conv + gsig[:, 2 * d:] * y_mem
    return x + _dot(m.astype(BF16), w_out)


def _mixers_kernel(x_ref, o_ref, mk_ref, mv_ref, ga_ref, wb_ref, wdw_ref, bdw_ref, gln_ref, bln_ref, wco_ref,
                   gqm_ref, wom_ref, womla_ref, wout_ref, x1_ref, conv_ref, ubuf):
    j = pl.program_id(1)
    tm = x_ref.shape[0]
    x = x_ref[...]
    h = _rms(x, ga_ref[...]).astype(BF16)
    pb = _dot(h, wb_ref[...])
    u = pb[:, :C_CONV] * _sigmoid(pb[:, C_CONV:2 * C_CONV])
    qm = pb[:, 2 * C_CONV:2 * C_CONV + MEM_W]
    gsig = _sigmoid(pb[:, 2 * C_CONV + MEM_W:])

    @pl.when(j == 0)
    def _():
        ubuf[0:CONV_TAIL, :] = jnp.zeros((CONV_TAIL, C_CONV), F32)

    ubuf[CONV_TAIL:CONV_TAIL + tm, :] = u
    y = jnp.zeros((tm, C_CONV), F32) + bdw_ref[...]
    off = CONV_TAIL - (CONV_W - 1)
    for tap in range(CONV_W):
        y = y + wdw_ref[tap:tap + 1, :] * ubuf[off + tap:off + tap + tm, :]
    tail = ubuf[tm:tm + CONV_TAIL, :]
    ubuf[0:CONV_TAIL, :] = tail

    @pl.when(j == pl.num_programs(1) - 1)
    def _():
        conv_ref[0] = tail[CONV_TAIL - (CONV_W - 1):, :]

    y_conv = _dot(_ln_silu(y, gln_ref[...], bln_ref[...]).astype(BF16), wco_ref[...])

    mk = mk_ref[0].astype(BF16)
    mv = mv_ref[0].astype(BF16)
    heads = []
    for hd in range(MEM_HEADS):
        sl = slice(hd * MEM_HD, (hd + 1) * MEM_HD)
        qh = _rms(qm[:, sl], gqm_ref[...]).astype(BF16)
        s = _dot_nt(qh, mk[:, sl]) * MEM_SCALE
        e = jnp.exp(s - jnp.max(s, axis=-1, keepdims=True))
        p = e / jnp.sum(e, axis=-1, keepdims=True)
        heads.append(_dot(p.astype(BF16), mv[:, sl]))
    y_mem = _dot(jnp.concatenate(heads, axis=-1).astype(BF16), wom_ref[...])

    y_mla = _dot(o_ref[...], womla_ref[...])
    x1_ref[...] = _merge_out(x, gsig, y_mla, y_conv, y_mem, wout_ref[...])


def _mixers(x2d, o_attn, mk, mv, batch, seq, w):
    t, d = x2d.shape
    tm = min(256, seq)
    ns = seq // tm
    nb = w["w_b"].shape[1]
    mtok = mk.shape[1]
    row = lambda b, j: (b * ns + j, 0)
    return pl.pallas_call(
        _mixers_kernel,
        grid=(batch, ns),
        in_specs=[pl.BlockSpec((tm, d), row), pl.BlockSpec((tm, MLA_HEADS * V_HEAD), row),
                  pl.BlockSpec((1, mtok, MEM_W), lambda b, j: (b, 0, 0)),
                  pl.BlockSpec((1, mtok, MEM_W), lambda b, j: (b, 0, 0)),
                  _const_spec((1, d)), _const_spec((d, nb)), _const_spec((CONV_W, C_CONV)),
                  _const_spec((1, C_CONV)), _const_spec((1, C_CONV)), _const_spec((1, C_CONV)),
                  _const_spec((C_CONV, d)), _const_spec((1, MEM_HD)), _const_spec((MEM_W, d)),
                  _const_spec((MLA_HEADS * V_HEAD, d)), _const_spec((d, d))],
        out_specs=[pl.BlockSpec((tm, d), row),
                   pl.BlockSpec((1, CONV_W - 1, C_CONV), lambda b, j: (b, 0, 0))],
        out_shape=[jax.ShapeDtypeStruct((t, d), F32), jax.ShapeDtypeStruct((batch, CONV_W - 1, C_CONV), F32)],
        scratch_shapes=[pltpu.VMEM((tm + CONV_TAIL, C_CONV), F32)],
        compiler_params=_params(("parallel", "arbitrary"), 56),
        name="token_mixers",
    )(x2d, o_attn, mk, mv, w["g_attn"], w["w_b"], w["w_dw"], w["b_dw"], w["g_conv_ln"], w["b_conv_ln"],
      w["w_conv_out"], w["g_q_mem"], w["w_o_mem"], w["w_o_mla"], w["w_out"])


def _ffn_prep_kernel(x1_ref, g_ref, wrh_ref, wrl_ref, br_ref, wsg_ref, wsu_ref, wsd_ref,
                     base_ref, h2u_ref, idx_ref, w_ref):
    x1 = x1_ref[...]
    d = x1.shape[-1]
    h2 = _rms(x1, g_ref[...])
    hh, hl = _split(h2)
    wrh = wrh_ref[...]
    logits = _dot_nt(wrh, hh) + _dot_nt(wrh, hl) + _dot_nt(wrl_ref[...], hh)
    scores = _sigmoid(logits)
    val = scores + br_ref[...]
    eio = lax.broadcasted_iota(I32, val.shape, 0).astype(F32)
    idxs, ws = [], []
    for _ in range(TOP_K):
        m = jnp.max(val, axis=0, keepdims=True)
        sel = jnp.min(jnp.where(val == m, eio, float(N_EXPERTS)), axis=0, keepdims=True)
        hit = eio == sel
        ws.append(jnp.sum(jnp.where(hit, scores, 0.0), axis=0, keepdims=True))
        idxs.append(sel)
        val = jnp.where(hit, -jnp.inf, val)
    wk = jnp.concatenate(ws, axis=0)
    idx_ref[...] = jnp.concatenate(idxs, axis=0).astype(I32)
    w_ref[...] = wk / jnp.sum(wk, axis=0, keepdims=True) * ROUTED_SCALE

    a = _silu(_dot(hh, wsg_ref[...])) * _dot(hh, wsu_ref[...])
    base_ref[...] = x1 + _dot(a.astype(BF16), wsd_ref[...])

    bits = lax.bitcast_convert_type(hh.astype(F32), U32)
    lo = lax.shift_right_logical(bits[:, :d // 2], jnp.uint32(16))
    hi = bits[:, d // 2:] & jnp.uint32(0xFFFF0000)
    h2u_ref[...] = hi | lo


def _ffn_prep(x1, w, tm):
    t, d = x1.shape
    ds = w["w_s_gate"].shape[1]
    row = lambda i: (i, 0)
    col = lambda i: (0, i)
    return pl.pallas_call(
        _ffn_prep_kernel,
        grid=(t // tm,),
        in_specs=[pl.BlockSpec((tm, d), row), _const_spec((1, d)), _const_spec((N_EXPERTS, d)),
                  _const_spec((N_EXPERTS, d)), _const_spec((N_EXPERTS, 1)), _const_spec((d, ds)),
                  _const_spec((d, ds)), _const_spec((ds, d))],
        out_specs=[pl.BlockSpec((tm, d), row), pl.BlockSpec((tm, d // 2), row),
                   pl.BlockSpec((TOP_K, tm), col), pl.BlockSpec((TOP_K, tm), col)],
        out_shape=[jax.ShapeDtypeStruct((t, d), F32), jax.ShapeDtypeStruct((t, d // 2), U32),
                   jax.ShapeDtypeStruct((TOP_K, t), I32), jax.ShapeDtypeStruct((TOP_K, t), F32)],
        compiler_params=_params(("parallel",), 40),
        name="ffn_prep",
    )(x1, w["g_ffn"], w["w_r_hi"], w["w_r_lo"], w["b_router"], w["w_s_gate"], w["w_s_up"], w["w_s_down"])


def _route_kernel(idx_ref, tri_ref, low_ref, dest_ref, pos_ref, rstart_ref, rcnt_ref, blk_ref, nvalid_ref,
                  cnt_scr, run_scr, *, blk):
    p = pl.program_id(0)
    i = pl.program_id(1)
    tm = idx_ref.shape[1]
    nsub = tm // ROUTE_SUB
    idx = idx_ref[...]
    eio = lax.broadcasted_iota(I32, (N_EXPERTS, ROUTE_SUB), 0)
    onehots, run_rows = [], []
    for s in range(nsub):
        ids = idx[:, s * ROUTE_SUB:(s + 1) * ROUTE_SUB]
        oh = jnp.zeros((N_EXPERTS, ROUTE_SUB), F32)
        for k in range(TOP_K):
            oh = oh + (eio == ids[k:k + 1, :]).astype(F32)
        onehots.append(oh)
        cnt_s = jnp.sum(oh, axis=1, keepdims=True)
        run_rows.append(jnp.ceil(cnt_s * (1.0 / RUN_ALIGN)) * RUN_ALIGN)

    @pl.when((p == 0) & (i == 0))
    def _():
        cnt_scr[...] = jnp.zeros(cnt_scr.shape, F32)

    @pl.when(p == 0)
    def _():
        cnt_scr[...] += sum(run_rows)

    @pl.when((p == 1) & (i == 0))
    def _():
        cnt = cnt_scr[...]
        padded = jnp.ceil(cnt * (1.0 / blk)) * blk
        r = lax.broadcasted_iota(I32, (N_EXPERTS, N_EXPERTS), 0)
        c = lax.broadcasted_iota(I32, (N_EXPERTS, N_EXPERTS), 1)
        pb = jnp.broadcast_to(padded, (N_EXPERTS, N_EXPERTS))
        prow = jnp.sum(jnp.where(r == c, pb, 0.0), axis=0, keepdims=True)
        prb = jnp.broadcast_to(prow, (N_EXPERTS, N_EXPERTS))
        pstart = jnp.sum(jnp.where(c < r, prb, 0.0), axis=1, keepdims=True)
        run_scr[...] = pstart
        pend = pstart + padded
        nb = blk_ref.shape[1]
        bstart = (lax.broadcasted_iota(I32, (N_EXPERTS, nb), 1) * blk).astype(F32)
        pend_b = jnp.broadcast_to(pend, (N_EXPERTS, nb))
        pstart_b = jnp.broadcast_to(pstart, (N_EXPERTS, nb))
        be = jnp.sum((pend_b <= bstart).astype(F32), axis=0, keepdims=True)
        blk_ref[...] = jnp.minimum(be, N_EXPERTS - 1.0).astype(I32)
        inside = (pstart_b <= bstart) & (bstart < pend_b)
        rows = jnp.clip(jnp.broadcast_to(cnt, (N_EXPERTS, nb)) - (bstart - pstart_b), 0.0, float(blk))
        nvalid_ref[...] = jnp.sum(jnp.where(inside, rows, 0.0), axis=0, keepdims=True).astype(I32)

    @pl.when(p == 1)
    def _():
        lane = lax.broadcasted_iota(I32, (N_EXPERTS, LANES), 1)
        rstart = jnp.zeros((N_EXPERTS, LANES), F32)
        rcnt = jnp.zeros((N_EXPERTS, LANES), F32)
        start = run_scr[...]
        for s in range(nsub):
            ids = idx[:, s * ROUTE_SUB:(s + 1) * ROUTE_SUB]
            within = _dot(onehots[s].astype(BF16), tri_ref[...])
            rows_b = jnp.broadcast_to(run_rows[s], (N_EXPERTS, ROUTE_SUB)).astype(BF16)
            local = _dot(low_ref[...], rows_b)
            d_rows, p_rows = [], []
            for k in range(TOP_K):
                hit = eio == ids[k:k + 1, :]
                d_rows.append(jnp.sum(jnp.where(hit, start + within, 0.0), axis=0, keepdims=True))
                p_rows.append(jnp.sum(jnp.where(hit, local + within, 0.0), axis=0, keepdims=True))
            sl = slice(s * ROUTE_SUB, (s + 1) * ROUTE_SUB)
            dest_ref[:, sl] = jnp.concatenate(d_rows, axis=0).astype(I32)
            pos_ref[:, sl] = jnp.concatenate(p_rows, axis=0).astype(I32)
            rstart = jnp.where(lane == s, start, rstart)
            rcnt = jnp.where(lane == s, run_rows[s], rcnt)
            start = start + run_rows[s]
        rstart_ref[0] = rstart.astype(I32)
        rcnt_ref[0] = rcnt.astype(I32)
        run_scr[...] = start


def _route(idx_pad, blk, nb_pad):
    tpad = idx_pad.shape[1]
    tm = ROUTE_TILE
    nsteps = tpad // tm
    sub = ROUTE_SUB
    tri = (lax.broadcasted_iota(I32, (sub, sub), 0) < lax.broadcasted_iota(I32, (sub, sub), 1)).astype(BF16)
    low = (lax.broadcasted_iota(I32, (N_EXPERTS, N_EXPERTS), 1)
           < lax.broadcasted_iota(I32, (N_EXPERTS, N_EXPERTS), 0)).astype(BF16)
    tok = lambda p, i: (0, i * p)
    run = lambda p, i: (i * p, 0, 0)
    return pl.pallas_call(
        functools.partial(_route_kernel, blk=blk),
        grid=(2, nsteps),
        in_specs=[pl.BlockSpec((TOP_K, tm), lambda p, i: (0, i)), _const_spec((sub, sub)),
                  _const_spec((N_EXPERTS, N_EXPERTS))],
        out_specs=[pl.BlockSpec((TOP_K, tm), tok), pl.BlockSpec((TOP_K, tm), tok),
                   pl.BlockSpec((1, N_EXPERTS, LANES), run), pl.BlockSpec((1, N_EXPERTS, LANES), run),
                   _const_spec((1, nb_pad)), _const_spec((1, nb_pad))],
        out_shape=[jax.ShapeDtypeStruct((TOP_K, tpad), I32), jax.ShapeDtypeStruct((TOP_K, tpad), I32),
                   jax.ShapeDtypeStruct((nsteps, N_EXPERTS, LANES), I32),
                   jax.ShapeDtypeStruct((nsteps, N_EXPERTS, LANES), I32),
                   jax.ShapeDtypeStruct((1, nb_pad), I32), jax.ShapeDtypeStruct((1, nb_pad), I32)],
        scratch_shapes=[pltpu.VMEM((N_EXPERTS, 1), F32), pltpu.VMEM((N_EXPERTS, 1), F32)],
        compiler_params=_params(("arbitrary", "arbitrary"), 32),
        name="moe_route",
    )(idx_pad, tri, low)


def _dispatch_kernel(dest_ref, h2u_ref, xs_in_ref, xs_ref, sem):
    del xs_in_ref
    tm = dest_ref.shape[1]

    def row_copy(t, k):
        return pltpu.make_async_copy(h2u_ref.at[pl.ds(t, 1)], xs_ref.at[pl.ds(dest_ref[k, t], 1)], sem)

    def body(t, carry):
        for k in range(TOP_K):
            row_copy(t, k).start()
        return carry

    lax.fori_loop(0, tm, body, 0)

    def drain(t, carry):
        for k in range(TOP_K):
            row_copy(t, k).wait()
        return carry

    lax.fori_loop(0, tm, drain, 0)


def _dispatch(dest, h2u, xs, tm):
    t, half = h2u.shape
    return pl.pallas_call(
        _dispatch_kernel,
        grid=(t // tm,),
        in_specs=[pl.BlockSpec((TOP_K, tm), lambda i: (0, i), memory_space=pltpu.SMEM),
                  pl.BlockSpec((tm, half), lambda i: (i, 0)), pl.BlockSpec(memory_space=pl.ANY)],
        out_specs=pl.BlockSpec(memory_space=pl.ANY),
        out_shape=jax.ShapeDtypeStruct(xs.shape, xs.dtype),
        scratch_shapes=[pltpu.SemaphoreType.DMA(())],
        input_output_aliases={2: 0},
        compiler_params=pltpu.CompilerParams(dimension_semantics=("arbitrary",), has_side_effects=True),
        name="moe_dispatch",
    )(dest, h2u, xs)


def _unpack_pairs(word):
    lo = lax.bitcast_convert_type(lax.shift_left(word, jnp.uint32(16)), F32).astype(BF16)
    hi = lax.bitcast_convert_type(word & jnp.uint32(0xFFFF0000), F32).astype(BF16)
    return lo, hi


def _pack_pairs(x):
    half = x.shape[1] // 2
    bits = lax.bitcast_convert_type(x.astype(BF16).astype(F32), U32)
    return (bits[:, half:] & jnp.uint32(0xFFFF0000)) | lax.shift_right_logical(bits[:, :half], jnp.uint32(16))


def _expert_kernel(be_ref, bm_ref, xs_ref, wg_ref, wu_ref, wd_ref, y_ref):
    b = pl.program_id(0)
    half = xs_ref.shape[1]

    @pl.when(bm_ref[b] == b)
    def _():
        lo, hi = _unpack_pairs(xs_ref[...])
        g = _dot(lo, wg_ref[0, :half, :]) + _dot(hi, wg_ref[0, half:, :])
        u = _dot(lo, wu_ref[0, :half, :]) + _dot(hi, wu_ref[0, half:, :])
        y_ref[...] = _pack_pairs(_dot((_silu(g) * u).astype(BF16), wd_ref[0]))


def _experts(block_e, block_map, xs, wg, wu, wd, blk):
    nslots, half = xs.shape
    d = 2 * half
    de = wg.shape[2]
    grid_spec = pltpu.PrefetchScalarGridSpec(
        num_scalar_prefetch=2,
        grid=(nslots // blk,),
        in_specs=[pl.BlockSpec((blk, half), lambda b, be, bm: (bm[b], 0)),
                  pl.BlockSpec((1, d, de), lambda b, be, bm: (be[bm[b]], 0, 0)),
                  pl.BlockSpec((1, d, de), lambda b, be, bm: (be[bm[b]], 0, 0)),
                  pl.BlockSpec((1, de, d), lambda b, be, bm: (be[bm[b]], 0, 0))],
        out_specs=pl.BlockSpec((blk, half), lambda b, be, bm: (bm[b], 0)),
    )
    return pl.pallas_call(
        _expert_kernel,
        grid_spec=grid_spec,
        out_shape=jax.ShapeDtypeStruct((nslots, half), U32),
        compiler_params=_params(("arbitrary",), 32),
        name="moe_experts",
    )(block_e, block_map, xs, wg, wu, wd)


def _combine_kernel(rs_ref, rc_ref, pos_ref, w_ref, base_ref, y_ref, out_ref, ybuf, sem, *, tile0):
    i = pl.program_id(0)
    n = pl.num_programs(0)
    tm, d = base_ref.shape
    half = d // 2
    nrow = ybuf.shape[1]

    def run_copy(tile, slot, e, off):
        rows = pl.multiple_of(rc_ref[tile * N_EXPERTS + e], RUN_ALIGN)
        src = y_ref.at[pl.ds(pl.multiple_of(rs_ref[tile * N_EXPERTS + e], RUN_ALIGN), rows)]
        dst = ybuf.at[slot, pl.ds(pl.multiple_of(off, RUN_ALIGN), rows)]
        return rows, pltpu.make_async_copy(src, dst, sem.at[slot])

    def for_runs(tile, slot, act):
        def body(e, off):
            rows, cp = run_copy(tile, slot, e, off)

            @pl.when(rows > 0)
            def _():
                act(cp)
            return off + rows
        lax.fori_loop(0, N_EXPERTS, body, 0)

    @pl.when(i == 0)
    def _():
        ybuf[...] = jnp.zeros(ybuf.shape, U32)
        for_runs(tile0, 0, lambda cp: cp.start())

    @pl.when(i + 1 < n)
    def _():
        for_runs(tile0 + i + 1, (i + 1) % 2, lambda cp: cp.start())

    slot = i % 2
    for_runs(tile0 + i, slot, lambda cp: cp.wait())

    pos = pos_ref[...]
    w = w_ref[...]
    col = lax.broadcasted_iota(I32, (tm, nrow), 1)
    pmat = jnp.zeros((tm, nrow), F32)
    for k in range(TOP_K):
        pmat = pmat + jnp.where(col == pos[:, k:k + 1], w[:, k:k + 1], 0.0)
    p_hi, p_lo = _split(pmat)
    y_lo, y_hi = _unpack_pairs(ybuf[slot])
    base = base_ref[...]
    out_ref[:, :half] = base[:, :half] + _dot(p_hi, y_lo) + _dot(p_lo, y_lo)
    out_ref[:, half:] = base[:, half:] + _dot(p_hi, y_hi) + _dot(p_lo, y_hi)


def _combine(run_start, run_rows, pos_t, w_t, base, y, tile0):
    t, d = base.shape
    tm = ROUTE_SUB
    nrow = -(-(TOP_K * tm + N_EXPERTS * (RUN_ALIGN - 1)) // LANES) * LANES
    grid_spec = pltpu.PrefetchScalarGridSpec(
        num_scalar_prefetch=2,
        grid=(t // tm,),
        in_specs=[pl.BlockSpec((tm, TOP_K), lambda i, rs, rc: (i, 0)),
                  pl.BlockSpec((tm, TOP_K), lambda i, rs, rc: (i, 0)),
                  pl.BlockSpec((tm, d), lambda i, rs, rc: (i, 0)),
                  pl.BlockSpec(memory_space=pl.ANY)],
        out_specs=pl.BlockSpec((tm, d), lambda i, rs, rc: (i, 0)),
        scratch_shapes=[pltpu.VMEM((2, nrow, d // 2), U32), pltpu.SemaphoreType.DMA((2,))],
    )
    return pl.pallas_call(
        functools.partial(_combine_kernel, tile0=tile0),
        grid_spec=grid_spec,
        out_shape=jax.ShapeDtypeStruct((t, d), F32),
        compiler_params=_params(("arbitrary",), 40),
        name="moe_combine",
    )(run_start, run_rows, pos_t, w_t, base, y)


def _sample_in_kernel(x_ref, ga_ref, wa_ref, gq_ref, wuq_ref, gqm_ref, gkv_ref, wuk_ref, e32_ref, gkm_ref,
                      gkn_ref, cos_ref, sa_ref, sb_ref, wb_ref, st_ref, wdw_ref, bdw_ref, gln_ref, bln_ref,
                      wco_ref, gqmem_ref,
                      c_ref, kr_ref, u_ref, qmn_ref, gsig_ref, yconv_ref, qabs_ref, qrope_ref, snew_ref):
    h, q_raw, k_raw, c, _, kr = _qkv_core(x_ref[...], ga_ref[...], wa_ref[...], gq_ref[...], wuq_ref[...],
                                          gqm_ref[...], gkv_ref[...], wuk_ref[...], e32_ref[...], gkm_ref[...])
    c_ref[...] = c
    kr_ref[...] = kr
    cos, sa, sb = cos_ref[...], sa_ref[...], sb_ref[...]
    for hd in range(MLA_HEADS):
        sl = slice(hd * HEAD_PAD, (hd + 1) * HEAD_PAD)
        qh = _rope128(_head_norm(q_raw[:, sl], gqm_ref[:, sl]), cos, sa, sb)
        kh = _rope128(_head_norm(k_raw[:, sl], gkm_ref[:, sl]), cos, sa, sb)
        snew_ref[:, hd:hd + 1] = jnp.sum(qh * kh, axis=-1, keepdims=True) * MLA_SCALE
        qrope_ref[hd] = qh[:, QK_NOPE:QK_HEAD]
        g_hi, g_lo = _split(qh * gkn_ref[:, sl])
        wk = wuk_ref[:, sl]
        qabs_ref[hd] = _dot_nt(g_hi, wk) + _dot_nt(g_lo, wk)

    pb = _dot(h, wb_ref[...])
    u = pb[:, :C_CONV] * _sigmoid(pb[:, C_CONV:2 * C_CONV])
    u_ref[...] = u
    for hd in range(MEM_HEADS):
        sl = slice(2 * C_CONV + hd * MEM_HD, 2 * C_CONV + (hd + 1) * MEM_HD)
        qmn_ref[:, hd * MEM_HD:(hd + 1) * MEM_HD] = _rms(pb[:, sl], gqmem_ref[...])
    gsig_ref[...] = _sigmoid(pb[:, 2 * C_CONV + MEM_W:])

    y = bdw_ref[...] + wdw_ref[CONV_W - 1:CONV_W, :] * u
    for tap in range(CONV_W - 1):
        y = y + wdw_ref[tap:tap + 1, :] * st_ref[tap]
    yconv_ref[...] = _dot(_ln_silu(y, gln_ref[...], bln_ref[...]).astype(BF16), wco_ref[...])


def _sample_in(xs2d, state_t, w, tabs):
    db, d = xs2d.shape
    ins = [xs2d, w["g_attn"], w["w_a"], w["g_q_lora"], w["w_uq_p"], w["g_q_mla_p"], w["g_kv_lora"], w["w_uk_p"],
           w["e32"], w["g_k_mla_p"], w["g_k_nope_p"], *tabs, w["w_b"], state_t, w["w_dw"], w["b_dw"],
           w["g_conv_ln"], w["b_conv_ln"], w["w_conv_out"], w["g_q_mem"]]
    outs = [jax.ShapeDtypeStruct((db, KV_LORA), F32), jax.ShapeDtypeStruct((db, QK_ROPE), F32),
            jax.ShapeDtypeStruct((db, C_CONV), F32), jax.ShapeDtypeStruct((db, MEM_W), F32),
            jax.ShapeDtypeStruct((db, N_BRANCH * d), F32), jax.ShapeDtypeStruct((db, d), F32),
            jax.ShapeDtypeStruct((MLA_HEADS, db, KV_LORA), F32), jax.ShapeDtypeStruct((MLA_HEADS, db, QK_ROPE), F32),
            jax.ShapeDtypeStruct((db, MLA_HEADS), F32)]
    return pl.pallas_call(
        _sample_in_kernel,
        grid=(1,),
        in_specs=[_const_spec(a.shape) for a in ins],
        out_specs=[_const_spec(o.shape) for o in outs],
        out_shape=outs,
        compiler_params=_params(("arbitrary",), 56),
        name="sample_in",
    )(*ins)


def _sample_mla_kernel(pt_ref, qabs_ref, qrope_ref, wukt_ref, g1_ref, g2_ref, cost_ref, sint_ref,
                       lat_ref, kro_ref, ctx_ref, m_ref, l_ref, cbuf, krbuf, lhs, sem_c, sem_k, *, n_chunks, ch):
    b = pl.program_id(0)
    nseq = pl.num_programs(0)
    page = cbuf.shape[2]
    pc = ch * page
    nrow = MLA_HEADS * QK_NOPE

    def page_copies(seq, chunk, slot, p):
        pid = pt_ref[seq, chunk * ch + p]
        return (pltpu.make_async_copy(lat_ref.at[pid], cbuf.at[slot, p], sem_c.at[slot]),
                pltpu.make_async_copy(kro_ref.at[pid], krbuf.at[slot, p], sem_k.at[slot]))

    def issue(seq, chunk, slot):
        for p in range(ch):
            a, k = page_copies(seq, chunk, slot, p)
            a.start()
            k.start()

    @pl.when(b == 0)
    def _():
        lhs[0:nrow, :] = wukt_ref[...]
        issue(0, 0, 0)

    qa = jnp.concatenate([qabs_ref[0], jnp.zeros((16 - MLA_HEADS, KV_LORA), F32)], axis=0)
    lhs[nrow:nrow + 16, :] = qa.astype(BF16)
    qr = qrope_ref[0].astype(BF16)
    g1 = g1_ref[...]
    g2 = g2_ref[...]

    def chunk_body(c, carry):
        m_prev, l_prev, ctx = carry
        slot = c % 2

        @pl.when(c + 1 < n_chunks)
        def _():
            issue(b, c + 1, 1 - slot)

        @pl.when((c + 1 == n_chunks) & (b + 1 < nseq))
        def _():
            issue(b + 1, 0, 1 - slot)

        for p in range(ch):
            a, k = page_copies(b, c, slot, p)
            a.wait()
            k.wait()

        cb = cbuf[slot].reshape(pc, KV_LORA).astype(BF16)
        kt = _dot_nt(lhs[...], cb)
        ssn = jnp.concatenate(
            [jnp.sum(jnp.square(kt[h * QK_NOPE:(h + 1) * QK_NOPE]), axis=0, keepdims=True)
             for h in range(MLA_HEADS)], axis=0)
        s_nope = kt[nrow:nrow + MLA_HEADS]

        krt = jnp.concatenate([krbuf[slot, p] for p in range(ch)], axis=1)
        ssr = jnp.sum(krt * krt, axis=0, keepdims=True)
        half = QK_ROPE // 2
        rot = jnp.concatenate([-krt[half:], krt[:half]], axis=0)
        roped = krt * g1 * cost_ref[c] + rot * g2 * sint_ref[c]
        s_rope = _dot(qr, roped.astype(BF16))

        rinv = lax.rsqrt((ssn + ssr) * (1.0 / QK_HEAD) + EPS)
        s = (s_nope + s_rope) * rinv * MLA_SCALE
        m_new = jnp.maximum(m_prev, jnp.max(s, axis=-1, keepdims=True))
        alpha = jnp.exp(m_prev - m_new)
        pexp = jnp.exp(s - m_new)
        l_new = alpha * l_prev + jnp.sum(pexp, axis=-1, keepdims=True)
        ctx_new = alpha * ctx + _dot(pexp.astype(BF16), cb)
        return m_new, l_new, ctx_new

    init = (jnp.full((MLA_HEADS, 1), NEG_BIG, F32), jnp.zeros((MLA_HEADS, 1), F32),
            jnp.zeros((MLA_HEADS, KV_LORA), F32))
    m_fin, l_fin, ctx = lax.fori_loop(0, n_chunks, chunk_body, init)
    ctx_ref[0] = ctx
    m_ref[0] = jnp.broadcast_to(m_fin, (MLA_HEADS, LANES))
    l_ref[0] = jnp.broadcast_to(l_fin, (MLA_HEADS, LANES))


def _sample_mla(page_table, qabs, qrope, lat, kro, w, ch):
    db, n_pages = page_table.shape
    page = lat.shape[1]
    n_chunks = n_pages // ch
    pc = ch * page
    nrow = MLA_HEADS * QK_NOPE
    grid_spec = pltpu.PrefetchScalarGridSpec(
        num_scalar_prefetch=1,
        grid=(db,),
        in_specs=[pl.BlockSpec((1, MLA_HEADS, KV_LORA), lambda b, pt: (b, 0, 0)),
                  pl.BlockSpec((1, MLA_HEADS, QK_ROPE), lambda b, pt: (b, 0, 0)),
                  pl.BlockSpec((nrow, KV_LORA), lambda b, pt: (0, 0)),
                  pl.BlockSpec((QK_ROPE, 1), lambda b, pt: (0, 0)),
                  pl.BlockSpec((QK_ROPE, 1), lambda b, pt: (0, 0)),
                  pl.BlockSpec((n_chunks, QK_ROPE, pc), lambda b, pt: (0, 0, 0)),
                  pl.BlockSpec((n_chunks, QK_ROPE, pc), lambda b, pt: (0, 0, 0)),
                  pl.BlockSpec(memory_space=pl.ANY), pl.BlockSpec(memory_space=pl.ANY)],
        out_specs=[pl.BlockSpec((1, MLA_HEADS, KV_LORA), lambda b, pt: (b, 0, 0)),
                   pl.BlockSpec((1, MLA_HEADS, LANES), lambda b, pt: (b, 0, 0)),
                   pl.BlockSpec((1, MLA_HEADS, LANES), lambda b, pt: (b, 0, 0))],
        scratch_shapes=[pltpu.VMEM((2, ch, page, KV_LORA), F32), pltpu.VMEM((2, ch, QK_ROPE, page), F32),
                        pltpu.VMEM((nrow + 16, KV_LORA), BF16),
                        pltpu.SemaphoreType.DMA((2,)), pltpu.SemaphoreType.DMA((2,))],
    )
    return pl.pallas_call(
        functools.partial(_sample_mla_kernel, n_chunks=n_chunks, ch=ch),
        grid_spec=grid_spec,
        out_shape=[jax.ShapeDtypeStruct((db, MLA_HEADS, KV_LORA), F32),
                   jax.ShapeDtypeStruct((db, MLA_HEADS, LANES), F32),
                   jax.ShapeDtypeStruct((db, MLA_HEADS, LANES), F32)],
        compiler_params=_params(("arbitrary",), 56),
        name="sample_mla",
    )(page_table, qabs, qrope, w["w_uk_t"], w["g_kr1"], w["g_kr2"], w["cos_t"], w["sin_t"],
      lat, kro)


def _sample_mem_kernel(q_ref, k_ref, v_ref, o_ref):
    q = q_ref[0]
    s = jnp.sum(k_ref[0] * q, axis=-1, keepdims=True) * MEM_SCALE
    e = jnp.exp(s - jnp.max(s, axis=0, keepdims=True))
    p = e / jnp.sum(e, axis=0, keepdims=True)
    o_ref[0] = jnp.sum(p * v_ref[0], axis=0)


def _sample_mem(qmn, mk, mv):
    db, mtok = mk.shape[:2]
    q3 = qmn.reshape(db, MEM_HEADS, MEM_HD)
    kv_spec = pl.BlockSpec((1, mtok, MEM_HEADS, MEM_HD), lambda b: (b, 0, 0, 0))
    q_spec = pl.BlockSpec((1, MEM_HEADS, MEM_HD), lambda b: (b, 0, 0))
    return pl.pallas_call(
        _sample_mem_kernel,
        grid=(db,),
        in_specs=[q_spec, kv_spec, kv_spec],
        out_specs=q_spec,
        out_shape=jax.ShapeDtypeStruct((db, MEM_HEADS, MEM_HD), F32),
        compiler_params=_params(("parallel",), 32),
        name="sample_mem",
    )(q3, mk, mv).reshape(db, MEM_W)


def _sample_out_kernel(x_ref, ctx_ref, m_ref, l_ref, snew_ref, c_ref, wuv_ref, womla_ref, omem_ref, wom_ref,
                       gsig_ref, yconv_ref, wout_ref, x1_ref):
    c_new = c_ref[...]
    m = m_ref[...]
    l = l_ref[...]
    s_new = snew_ref[...]
    d = x_ref.shape[-1]
    y_mla = jnp.zeros((x_ref.shape[0], d), F32)
    for h in range(MLA_HEADS):
        mh, lh, sh = m[:, h:h + 1], l[:, h:h + 1], s_new[:, h:h + 1]
        m_fin = jnp.maximum(mh, sh)
        a_old = jnp.exp(mh - m_fin)
        a_new = jnp.exp(sh - m_fin)
        denom = lh * a_old + a_new
        ctx = (ctx_ref[h] * a_old + a_new * c_new) / denom
        o_h = _dot(ctx.astype(BF16), wuv_ref[h])
        y_mla = y_mla + _dot(o_h.astype(BF16), womla_ref[h])
    y_mem = _dot(omem_ref[...].astype(BF16), wom_ref[...])
    x1_ref[...] = _merge_out(x_ref[...], gsig_ref[...], y_mla, yconv_ref[...], y_mem, wout_ref[...])


def _sample_out(xs2d, ctx_t, m, l, s_new, c_s, o_mem, gsig, y_conv, w):
    ins = [xs2d, ctx_t, m, l, s_new, c_s, w["w_uv_h"], w["w_o_mla_h"], o_mem, w["w_o_mem"], gsig, y_conv,
           w["w_out"]]
    return pl.pallas_call(
        _sample_out_kernel,
        grid=(1,),
        in_specs=[_const_spec(a.shape) for a in ins],
        out_specs=_const_spec(xs2d.shape),
        out_shape=jax.ShapeDtypeStruct(xs2d.shape, F32),
        compiler_params=_params(("arbitrary",), 40),
        name="sample_out",
    )(*ins)


def _prep_weights(g_attn_norm, w_in, g_q_lora, w_uq, g_q_mla, g_kv_lora, w_uk, w_uv, g_k_mla, w_o_mla, w_dw, b_dw,
                  g_conv_ln, b_conv_ln, w_conv_out, g_mem_norm, w_mem_kv, g_q_mem, g_k_mem, w_o_mem, w_out,
                  g_ffn_norm, w_router, b_router, w_e_gate, w_e_up, w_e_down, w_s_gate, w_s_up, w_s_down):
    row = lambda g: g.reshape(1, -1).astype(F32)
    e32 = jnp.zeros((QK_ROPE, MLA_HEADS, HEAD_PAD), F32)
    e32 = e32.at[jnp.arange(QK_ROPE), :, QK_NOPE + jnp.arange(QK_ROPE)].set(1.0)
    half = QK_ROPE // 2
    g_rope = g_k_mla[QK_NOPE:]
    w_r_hi = w_router.T.astype(BF16)
    w_r_lo = (w_router.T - w_r_hi.astype(F32)).astype(BF16)
    g_k_nope = jnp.concatenate([g_k_mla[:QK_NOPE], jnp.zeros((QK_ROPE,), F32)])
    return {
        "g_attn": row(g_attn_norm),
        "w_a": w_in[:, :QKV_COLS].astype(BF16),
        "w_b": w_in[:, QKV_COLS:].astype(BF16),
        "g_q_lora": row(g_q_lora),
        "w_uq_p": _pad_heads(w_uq, QK_HEAD).astype(BF16),
        "g_q_mla_p": _pad_heads(jnp.tile(g_q_mla, MLA_HEADS).reshape(1, -1), QK_HEAD),
        "g_kv_lora": row(g_kv_lora),
        "w_uk_p": _pad_heads(w_uk.reshape(KV_LORA, MLA_HEADS * QK_NOPE), QK_NOPE).astype(BF16),
        "e32": e32.reshape(QK_ROPE, MLA_HEADS * HEAD_PAD).astype(BF16),
        "g_k_mla_p": _pad_heads(jnp.tile(g_k_mla, MLA_HEADS).reshape(1, -1), QK_HEAD),
        "g_k_nope_p": _pad_heads(jnp.tile(g_k_nope, MLA_HEADS).reshape(1, -1), QK_HEAD),
        "w_uv_p": _pad_heads(w_uv.reshape(KV_LORA, MLA_HEADS * V_HEAD), V_HEAD).astype(BF16),
        "w_uv_h": w_uv.transpose(1, 0, 2).astype(BF16),
        "w_uk_t": w_uk.reshape(KV_LORA, MLA_HEADS * QK_NOPE).T.astype(BF16),
        "g_kr1": g_rope.reshape(QK_ROPE, 1),
        "g_kr2": jnp.concatenate([g_rope[half:], g_rope[:half]]).reshape(QK_ROPE, 1),
        "w_o_mla": w_o_mla.astype(BF16),
        "w_o_mla_h": w_o_mla.reshape(MLA_HEADS, V_HEAD, -1).astype(BF16),
        "w_dw": w_dw.astype(F32), "b_dw": row(b_dw), "g_conv_ln": row(g_conv_ln), "b_conv_ln": row(b_conv_ln),
        "w_conv_out": w_conv_out.astype(BF16),
        "g_mem_norm": row(g_mem_norm), "w_mem_kv": w_mem_kv.astype(BF16),
        "g_q_mem": row(g_q_mem), "g_k_mem": row(g_k_mem),
        "w_o_mem": w_o_mem.astype(BF16), "w_out": w_out.astype(BF16),
        "g_ffn": row(g_ffn_norm), "w_r_hi": w_r_hi, "w_r_lo": w_r_lo,
        "b_router": b_router.reshape(N_EXPERTS, 1).astype(F32),
        "w_e_gate": w_e_gate.astype(BF16), "w_e_up": w_e_up.astype(BF16), "w_e_down": w_e_down.astype(BF16),
        "w_s_gate": w_s_gate.astype(BF16), "w_s_up": w_s_up.astype(BF16), "w_s_down": w_s_down.astype(BF16),
    }


def _layer(xp, xs, mem_prompt, cache_latent, cache_krope, state_conv, cache_mem_k, cache_mem_v, page_table, w):
    b, s, d = xp.shape
    db, ds, _ = xs.shape
    assert ds == 1, "the sample group decodes one token per sequence"
    n_pages, page = page_table.shape[1], cache_latent.shape[1]
    past = n_pages * page
    mtok = mem_prompt.shape[1]
    xp2 = xp.reshape(b * s, d)
    xs2 = xs.reshape(db * ds, d)

    mk_p, mv_p = _mem_kv(mem_prompt.reshape(b * mtok, d), w["g_mem_norm"], w["w_mem_kv"], w["g_k_mem"])
    q, k, v, c_p, kr_p = _qkv(xp2, s, w, _rope_tables(jnp.arange(s)))
    o_attn = _flash(q, k, v, b, s)
    x1_p, conv_p = _mixers(xp2, o_attn, mk_p.reshape(b, mtok, MEM_W), mv_p.reshape(b, mtok, MEM_W), b, s, w)

    ch = math.gcd(n_pages, 16)
    pc = ch * page
    half = QK_ROPE // 2
    inv = ROPE_THETA ** (-jnp.arange(half, dtype=F32) / half)
    ang = inv[:, None] * jnp.arange(past, dtype=F32)[None, :]
    tab = lambda a: jnp.concatenate([a, a], axis=0).reshape(QK_ROPE, past // pc, pc).transpose(1, 0, 2)
    ws = dict(w, cos_t=tab(jnp.cos(ang)), sin_t=tab(jnp.sin(ang)))
    tabs_s = _rope_tables(jnp.full((db,), past, I32))
    (c_s, kr_s, u_s, qmn, gsig_s, yconv_s, qabs, qrope, s_new) = _sample_in(
        xs2, state_conv.transpose(1, 0, 2), w, tabs_s)
    ctx, m_s, l_s = _sample_mla(page_table, qabs.transpose(1, 0, 2), qrope.transpose(1, 0, 2),
                                cache_latent, cache_krope.transpose(0, 2, 1), ws, ch)
    o_mem = _sample_mem(qmn, cache_mem_k, cache_mem_v)
    x1_s = _sample_out(xs2, ctx.transpose(1, 0, 2), m_s[:, :, 0], l_s[:, :, 0], s_new, c_s, o_mem, gsig_s,
                       yconv_s, w)

    base_p, h2u_p, idx_p, wk_p = _ffn_prep(x1_p, w, min(512, b * s))
    base_s, h2u_s, idx_s, wk_s = _ffn_prep(x1_s, w, db)
    tp, tsm = b * s, db
    t_all = tp + tsm
    assert tp % ROUTE_SUB == 0 and tsm % ROUTE_SUB == 0, "token groups must fill whole routing tiles"
    blk = 512
    t_pad = -(-t_all // ROUTE_TILE) * ROUTE_TILE
    idx_all = jnp.concatenate([idx_p, idx_s, jnp.full((TOP_K, t_pad - t_all), N_EXPERTS, I32)], axis=1)
    n_tiles = t_all // ROUTE_SUB
    max_rows = t_all * TOP_K + n_tiles * N_EXPERTS * (RUN_ALIGN - 1) + N_EXPERTS * (blk - 1)
    n_blocks = -(-max_rows // blk)
    nb_pad = -(-n_blocks // LANES) * LANES
    dest, pos, run_start, run_rows, block_e, nvalid = _route(idx_all, blk, nb_pad)
    per_step = ROUTE_TILE // ROUTE_SUB
    flat = lambda a: a[:, :, :per_step].transpose(0, 2, 1).reshape(-1)
    run_start, run_rows = flat(run_start), flat(run_rows)
    n_used = jnp.sum((nvalid[0, :n_blocks] > 0).astype(I32))
    block_map = jnp.minimum(jnp.arange(n_blocks, dtype=I32), n_used - 1)
    xs_sorted = jnp.zeros((n_blocks * blk, d // 2), U32)
    xs_sorted = _dispatch(dest[:, :tp], h2u_p, xs_sorted, min(256, tp))
    xs_sorted = _dispatch(dest[:, tp:t_all], h2u_s, xs_sorted, tsm)
    y = _experts(block_e[0, :n_blocks], block_map, xs_sorted, w["w_e_gate"], w["w_e_up"], w["w_e_down"], blk)
    pos_t = pos.T
    yp = _combine(run_start, run_rows, pos_t[:tp], wk_p.T, base_p, y, 0)
    ys = _combine(run_start, run_rows, pos_t[tp:t_all], wk_s.T, base_s, y, tp // ROUTE_SUB)

    new_conv_s = jnp.concatenate([state_conv[:, 1:], u_s[:, None, :]], axis=1)
    return (yp.reshape(b, s, d), ys.reshape(db, ds, d), c_p.reshape(b, s, KV_LORA), kr_p.reshape(b, s, QK_ROPE),
            conv_p, mk_p.reshape(b, mtok, MEM_HEADS, MEM_HD), mv_p.reshape(b, mtok, MEM_HEADS, MEM_HD),
            c_s.reshape(db, ds, KV_LORA), kr_s.reshape(db, ds, QK_ROPE), new_conv_s)


def kernel(x_prompt, x_sample, mem_prompt, cache_latent, cache_krope, state_conv, cache_mem_k, cache_mem_v,
           page_table, g_attn_norm, w_in, g_q_lora, w_uq, g_q_mla, g_kv_lora, w_uk, w_uv, g_k_mla, w_o_mla, w_dw,
           b_dw, g_conv_ln, b_conv_ln, w_conv_out, g_mem_norm, w_mem_kv, g_q_mem, g_k_mem, w_o_mem, w_out,
           g_ffn_norm, w_router, b_router, w_e_gate, w_e_up, w_e_down, w_s_gate, w_s_up, w_s_down):
    depth = w_in.shape[0]
    params = (g_attn_norm, w_in, g_q_lora, w_uq, g_q_mla, g_kv_lora, w_uk, w_uv, g_k_mla, w_o_mla, w_dw, b_dw,
              g_conv_ln, b_conv_ln, w_conv_out, g_mem_norm, w_mem_kv, g_q_mem, g_k_mem, w_o_mem, w_out,
              g_ffn_norm, w_router, b_router, w_e_gate, w_e_up, w_e_down, w_s_gate, w_s_up, w_s_down)
    xp, xs = x_prompt, x_sample
    per_layer = []
    for layer in range(depth):
        w = _prep_weights(*[p[layer] for p in params])
        outs = _layer(xp, xs, mem_prompt, cache_latent[layer], cache_krope[layer], state_conv[layer],
                      cache_mem_k[layer], cache_mem_v[layer], page_table, w)
        xp, xs = outs[0], outs[1]
        per_layer.append(outs[2:])
    stacked = tuple(jnp.stack([pl_[i] for pl_ in per_layer]) for i in range(8))
    return (xp, xs) + stacked
```

```python
import functools
import math

import jax
import jax.numpy as jnp
from jax import lax
from jax.experimental import pallas as pl
from jax.experimental.pallas import tpu as pltpu

F32 = jnp.float32
BF16 = jnp.bfloat16
I32 = jnp.int32
U32 = jnp.uint32

MLA_HEADS = 8
QK_NOPE = 64
QK_ROPE = 32
QK_HEAD = QK_NOPE + QK_ROPE
V_HEAD = 64
Q_LORA = 256
KV_LORA = 256
ROPE_THETA = 10000.0
MLA_SCALE = QK_HEAD ** -0.5
C_CONV = 512
CONV_W = 31
MEM_HEADS = 4
MEM_HD = 128
MEM_W = MEM_HEADS * MEM_HD
MEM_SCALE = MEM_HD ** -0.5
N_BRANCH = 3
N_EXPERTS = 64
TOP_K = 8
D_EXPERT = 256
ROUTED_SCALE = 2.5
EPS = 1e-6
NEG_BIG = -1e30
Q_PRESCALE = MLA_SCALE * math.log2(math.e)

LANES = 128
HEAD_PAD = LANES
QKV_COLS = Q_LORA + KV_LORA + QK_ROPE
CONV_TAIL = 32
SUBLANES = 8
RUN_ALIGN = SUBLANES
ROUTE_SUB = 128
ROUTE_TILE = 512
MIB = 1024 * 1024


def _dot(a, b):
    return jnp.dot(a, b, preferred_element_type=F32)


def _dot_nt(a, b):
    return lax.dot_general(a, b, (((1,), (1,)), ((), ())), preferred_element_type=F32)


def _split(x):
    hi = x.astype(BF16)
    lo = (x - hi.astype(F32)).astype(BF16)
    return hi, lo


def _rms(x, g):
    return x * lax.rsqrt(jnp.mean(x * x, axis=-1, keepdims=True) + EPS) * g


def _head_norm(x, g):
    ss = jnp.sum(x * x, axis=-1, keepdims=True)
    return x * lax.rsqrt(ss * (1.0 / QK_HEAD) + EPS) * g


def _rope128(n, cos, sa, sb):
    return n * cos + pltpu.roll(n, LANES - QK_ROPE // 2, 1) * sa + pltpu.roll(n, QK_ROPE // 2, 1) * sb


def _sigmoid(x):
    return jax.nn.sigmoid(x)


def _silu(x):
    return x * jax.nn.sigmoid(x)


def _const_spec(shape):
    nd = len(shape)
    return pl.BlockSpec(shape, lambda *_: (0,) * nd)


def _params(sem, vmem_mib):
    return pltpu.CompilerParams(dimension_semantics=sem, vmem_limit_bytes=vmem_mib * MIB)


def _pad_heads(w, used):
    lead = w.shape[:-1]
    w = w.reshape(lead + (MLA_HEADS, used))
    w = jnp.pad(w, [(0, 0)] * len(lead) + [(0, 0), (0, HEAD_PAD - used)])
    return w.reshape(lead + (MLA_HEADS * HEAD_PAD,))


def _rope_tables(pos):
    half = QK_ROPE // 2
    inv = ROPE_THETA ** (-jnp.arange(half, dtype=F32) / half)
    ang = pos.astype(F32)[:, None] * inv[None, :]
    cos, sin = jnp.cos(ang), jnp.sin(ang)
    n = pos.shape[0]
    ones = jnp.ones((n, QK_NOPE), F32)
    zeros = jnp.zeros((n, QK_NOPE), F32)
    tail1 = jnp.ones((n, HEAD_PAD - QK_HEAD), F32)
    tail0 = jnp.zeros((n, HEAD_PAD - QK_HEAD), F32)
    z16 = jnp.zeros((n, half), F32)
    cosb = jnp.concatenate([ones, cos, cos, tail1], axis=1)
    sa = jnp.concatenate([zeros, -sin, z16, tail0], axis=1)
    sb = jnp.concatenate([zeros, z16, sin, tail0], axis=1)
    return cosb, sa, sb


def _mem_kv_kernel(mem_ref, gn_ref, w_ref, gk_ref, k_ref, v_ref):
    m = _rms(mem_ref[...], gn_ref[...]).astype(BF16)
    kv = _dot(m, w_ref[...])
    for h in range(MEM_HEADS):
        sl = slice(h * MEM_HD, (h + 1) * MEM_HD)
        k_ref[:, sl] = _rms(kv[:, sl], gk_ref[...])
    v_ref[...] = kv[:, MEM_W:]


def _mem_kv(mem2d, g_norm, w_kv_bf, g_k):
    n, d = mem2d.shape
    tm = 256
    return pl.pallas_call(
        _mem_kv_kernel,
        grid=(n // tm,),
        in_specs=[pl.BlockSpec((tm, d), lambda i: (i, 0)), _const_spec((1, d)), _const_spec((d, 2 * MEM_W)),
                  _const_spec((1, MEM_HD))],
        out_specs=[pl.BlockSpec((tm, MEM_W), lambda i: (i, 0)), pl.BlockSpec((tm, MEM_W), lambda i: (i, 0))],
        out_shape=[jax.ShapeDtypeStruct((n, MEM_W), F32), jax.ShapeDtypeStruct((n, MEM_W), F32)],
        compiler_params=_params(("parallel",), 32),
        name="mem_kv",
    )(mem2d, g_norm, w_kv_bf, g_k)


def _qkv_core(x, ga, wa, gq, wuq, gqm, gkv, wuk, e32, gkm):
    h = _rms(x, ga).astype(BF16)
    pa = _dot(h, wa)
    q_lat = pa[:, :Q_LORA]
    kv_lat = pa[:, Q_LORA:Q_LORA + KV_LORA]
    kr = pa[:, Q_LORA + KV_LORA:QKV_COLS]
    c_q = _rms(q_lat, gq).astype(BF16)
    q_raw = _dot(c_q, wuq)
    c = _rms(kv_lat, gkv)
    cb = c.astype(BF16)
    kr_hi, kr_lo = _split(kr)
    k_raw = _dot(cb, wuk) + _dot(kr_hi, e32) + _dot(kr_lo, e32)
    return h, q_raw, k_raw, c, cb, kr


def _qkv_kernel(x_ref, ga_ref, wa_ref, gq_ref, wuq_ref, gqm_ref, gkv_ref, wuk_ref, e32_ref, gkm_ref, wuv_ref,
                cos_ref, sa_ref, sb_ref, q_ref, k_ref, v_ref, c_ref, kr_ref):
    _, q_raw, k_raw, c, cb, kr = _qkv_core(x_ref[...], ga_ref[...], wa_ref[...], gq_ref[...], wuq_ref[...],
                                           gqm_ref[...], gkv_ref[...], wuk_ref[...], e32_ref[...], gkm_ref[...])
    c_ref[...] = c
    kr_ref[...] = kr
    lane = lax.broadcasted_iota(I32, (1, MLA_HEADS * HEAD_PAD), 1) % HEAD_PAD
    v_ref[...] = (_dot(cb, wuv_ref[...]) + (lane == V_HEAD).astype(F32)).astype(BF16)
    cos, sa, sb = cos_ref[...], sa_ref[...], sb_ref[...]
    for h in range(MLA_HEADS):
        sl = slice(h * HEAD_PAD, (h + 1) * HEAD_PAD)
        qh = _rope128(_head_norm(q_raw[:, sl], gqm_ref[:, sl]), cos, sa, sb)
        q_ref[:, sl] = (qh * Q_PRESCALE).astype(BF16)
        k_ref[:, sl] = _rope128(_head_norm(k_raw[:, sl], gkm_ref[:, sl]), cos, sa, sb).astype(BF16)


def _qkv(x2d, seq, w, tabs):
    t, d = x2d.shape
    tm = min(512, seq)
    nseq = seq // tm
    hp = MLA_HEADS * HEAD_PAD
    row = lambda i: (i, 0)
    tab = lambda i: (i % nseq, 0)
    return pl.pallas_call(
        _qkv_kernel,
        grid=(t // tm,),
        in_specs=[pl.BlockSpec((tm, d), row), _const_spec((1, d)), _const_spec((d, QKV_COLS)),
                  _const_spec((1, Q_LORA)), _const_spec((Q_LORA, hp)), _const_spec((1, hp)),
                  _const_spec((1, KV_LORA)), _const_spec((KV_LORA, hp)), _const_spec((QK_ROPE, hp)),
                  _const_spec((1, hp)), _const_spec((KV_LORA, hp)),
                  pl.BlockSpec((tm, HEAD_PAD), tab), pl.BlockSpec((tm, HEAD_PAD), tab),
                  pl.BlockSpec((tm, HEAD_PAD), tab)],
        out_specs=[pl.BlockSpec((tm, hp), row), pl.BlockSpec((tm, hp), row), pl.BlockSpec((tm, hp), row),
                   pl.BlockSpec((tm, KV_LORA), row), pl.BlockSpec((tm, QK_ROPE), row)],
        out_shape=[jax.ShapeDtypeStruct((t, hp), BF16), jax.ShapeDtypeStruct((t, hp), BF16),
                   jax.ShapeDtypeStruct((t, hp), BF16), jax.ShapeDtypeStruct((t, KV_LORA), F32),
                   jax.ShapeDtypeStruct((t, QK_ROPE), F32)],
        compiler_params=_params(("parallel",), 48),
        name="qkv_proj",
    )(x2d, w["g_attn"], w["w_a"], w["g_q_lora"], w["w_uq_p"], w["g_q_mla_p"], w["g_kv_lora"], w["w_uk_p"],
      w["e32"], w["g_k_mla_p"], w["w_uv_p"], *tabs)


def _flash_kernel(q_ref, k_ref, v_ref, o_ref, m_scr, acc_scr):
    i = pl.program_id(1)
    j = pl.program_id(2)
    tq = q_ref.shape[0]
    tk = k_ref.shape[0]
    last = i // 2

    @pl.when(j == 0)
    def _():
        m_scr[...] = jnp.full(m_scr.shape, NEG_BIG, F32)
        acc_scr[...] = jnp.zeros(acc_scr.shape, F32)

    def step(nkeys, shift):
        if shift is not None:
            row = lax.broadcasted_iota(I32, (tq, nkeys), 0)
            col = lax.broadcasted_iota(I32, (tq, nkeys), 1)
            keep = col <= row + shift
        for h in range(MLA_HEADS):
            sl = slice(h * HEAD_PAD, (h + 1) * HEAD_PAD)
            s = _dot_nt(q_ref[:, sl], k_ref[0:nkeys, sl])
            if shift is not None:
                s = jnp.where(keep, s, NEG_BIG)
            m_prev = m_scr[h]
            m_new = jnp.maximum(m_prev, jnp.max(s, axis=-1, keepdims=True))
            alpha = jnp.exp2(m_prev - m_new)
            p = jnp.exp2(s - jnp.tile(m_new, (1, nkeys // LANES)))
            acc_scr[h] = alpha * acc_scr[h] + _dot(p.astype(BF16), v_ref[0:nkeys, sl])
            m_scr[h] = m_new

    @pl.when(j < last)
    def _():
        step(tk, None)

    @pl.when((j == last) & (i % 2 == 0))
    def _():
        step(tq, 0)

    @pl.when((j == last) & (i % 2 == 1))
    def _():
        step(tk, tq)

    @pl.when(j == last)
    def _():
        for h in range(MLA_HEADS):
            acc = acc_scr[h]
            o = acc / acc[:, V_HEAD:V_HEAD + 1]
            o_ref[:, h * V_HEAD:(h + 1) * V_HEAD] = o[:, :V_HEAD].astype(BF16)


def _flash(q, k, v, batch, seq):
    t, hp = q.shape
    tq = min(512, seq // 2)
    tk = 2 * tq
    nq, nk = seq // tq, seq // tk
    qmap = lambda b, i, j: (b * nq + i, 0)
    kmap = lambda b, i, j: (b * nk + jnp.minimum(j, i // 2), 0)
    return pl.pallas_call(
        _flash_kernel,
        grid=(batch, nq, nk),
        in_specs=[pl.BlockSpec((tq, hp), qmap), pl.BlockSpec((tk, hp), kmap), pl.BlockSpec((tk, hp), kmap)],
        out_specs=pl.BlockSpec((tq, MLA_HEADS * V_HEAD), qmap),
        out_shape=jax.ShapeDtypeStruct((t, MLA_HEADS * V_HEAD), BF16),
        scratch_shapes=[pltpu.VMEM((MLA_HEADS, tq, LANES), F32), pltpu.VMEM((MLA_HEADS, tq, HEAD_PAD), F32)],
        compiler_params=_params(("parallel", "parallel", "arbitrary"), 56),
        name="flash_attn",
    )(q, k, v)


def _ln_silu(y, g, b):
    mu = jnp.mean(y, axis=-1, keepdims=True)
    yc = y - mu
    n = yc * lax.rsqrt(jnp.mean(yc * yc, axis=-1, keepdims=True) + EPS)
    return _silu(n * g + b)


def _merge_out(x, gsig, y_mla, y_conv, y_mem, w_out):
    d = x.shape[-1]
    m = gsig[:, :d] * y_mla + gsig[:, d:2 * d] * y_conv + gsig[:, 2 * d:] * y_mem
    return x + _dot(m.astype(BF16), w_out)


def _mixers_kernel(x_ref, o_ref, mk_ref, mv_ref, ga_ref, wb_ref, wdw_ref, bdw_ref, gln_ref, bln_ref, wco_ref,
                   gqm_ref, wom_ref, womla_ref, wout_ref, x1_ref, conv_ref, ubuf):
    j = pl.program_id(1)
    tm = x_ref.shape[0]
    x = x_ref[...]
    h = _rms(x, ga_ref[...]).astype(BF16)
    pb = _dot(h, wb_ref[...])
    u = pb[:, :C_CONV] * _sigmoid(pb[:, C_CONV:2 * C_CONV])
    qm = pb[:, 2 * C_CONV:2 * C_CONV + MEM_W]
    gsig = _sigmoid(pb[:, 2 * C_CONV + MEM_W:])

    @pl.when(j == 0)
    def _():
        ubuf[0:CONV_TAIL, :] = jnp.zeros((CONV_TAIL, C_CONV), F32)

    ubuf[CONV_TAIL:CONV_TAIL + tm, :] = u
    y = jnp.zeros((tm, C_CONV), F32) + bdw_ref[...]
    off = CONV_TAIL - (CONV_W - 1)
    for tap in range(CONV_W):
        y = y + wdw_ref[tap:tap + 1, :] * ubuf[off + tap:off + tap + tm, :]
    tail = ubuf[tm:tm + CONV_TAIL, :]
    ubuf[0:CONV_TAIL, :] = tail

    @pl.when(j == pl.num_programs(1) - 1)
    def _():
        conv_ref[0] = tail[CONV_TAIL - (CONV_W - 1):, :]

    y_conv = _dot(_ln_silu(y, gln_ref[...], bln_ref[...]).astype(BF16), wco_ref[...])

    mk = mk_ref[0].astype(BF16)
    mv = mv_ref[0].astype(BF16)
    heads = []
    for hd in range(MEM_HEADS):
        sl = slice(hd * MEM_HD, (hd + 1) * MEM_HD)
        qh = _rms(qm[:, sl], gqm_ref[...]).astype(BF16)
        s = _dot_nt(qh, mk[:, sl]) * MEM_SCALE
        e = jnp.exp(s - jnp.max(s, axis=-1, keepdims=True))
        p = e / jnp.sum(e, axis=-1, keepdims=True)
        heads.append(_dot(p.astype(BF16), mv[:, sl]))
    y_mem = _dot(jnp.concatenate(heads, axis=-1).astype(BF16), wom_ref[...])

    y_mla = _dot(o_ref[...], womla_ref[...])
    x1_ref[...] = _merge_out(x, gsig, y_mla, y_conv, y_mem, wout_ref[...])


def _mixers(x2d, o_attn, mk, mv, batch, seq, w):
    t, d = x2d.shape
    tm = min(256, seq)
    ns = seq // tm
    nb = w["w_b"].shape[1]
    mtok = mk.shape[1]
    row = lambda b, j: (b * ns + j, 0)
    return pl.pallas_call(
        _mixers_kernel,
        grid=(batch, ns),
        in_specs=[pl.BlockSpec((tm, d), row), pl.BlockSpec((tm, MLA_HEADS * V_HEAD), row),
                  pl.BlockSpec((1, mtok, MEM_W), lambda b, j: (b, 0, 0)),
                  pl.BlockSpec((1, mtok, MEM_W), lambda b, j: (b, 0, 0)),
                  _const_spec((1, d)), _const_spec((d, nb)), _const_spec((CONV_W, C_CONV)),
                  _const_spec((1, C_CONV)), _const_spec((1, C_CONV)), _const_spec((1, C_CONV)),
                  _const_spec((C_CONV, d)), _const_spec((1, MEM_HD)), _const_spec((MEM_W, d)),
                  _const_spec((MLA_HEADS * V_HEAD, d)), _const_spec((d, d))],
        out_specs=[pl.BlockSpec((tm, d), row),
                   pl.BlockSpec((1, CONV_W - 1, C_CONV), lambda b, j: (b, 0, 0))],
        out_shape=[jax.ShapeDtypeStruct((t, d), F32), jax.ShapeDtypeStruct((batch, CONV_W - 1, C_CONV), F32)],
        scratch_shapes=[pltpu.VMEM((tm + CONV_TAIL, C_CONV), F32)],
        compiler_params=_params(("parallel", "arbitrary"), 56),
        name="token_mixers",
    )(x2d, o_attn, mk, mv, w["g_attn"], w["w_b"], w["w_dw"], w["b_dw"], w["g_conv_ln"], w["b_conv_ln"],
      w["w_conv_out"], w["g_q_mem"], w["w_o_mem"], w["w_o_mla"], w["w_out"])


def _ffn_prep_kernel(x1_ref, g_ref, wrh_ref, wrl_ref, br_ref, wsg_ref, wsu_ref, wsd_ref,
                     base_ref, h2u_ref, idx_ref, w_ref):
    x1 = x1_ref[...]
    d = x1.shape[-1]
    h2 = _rms(x1, g_ref[...])
    hh, hl = _split(h2)
    wrh = wrh_ref[...]
    logits = _dot_nt(wrh, hh) + _dot_nt(wrh, hl) + _dot_nt(wrl_ref[...], hh)
    scores = _sigmoid(logits)
    val = scores + br_ref[...]
    eio = lax.broadcasted_iota(I32, val.shape, 0).astype(F32)
    idxs, ws = [], []
    for _ in range(TOP_K):
        m = jnp.max(val, axis=0, keepdims=True)
        sel = jnp.min(jnp.where(val == m, eio, float(N_EXPERTS)), axis=0, keepdims=True)
        hit = eio == sel
        ws.append(jnp.sum(jnp.where(hit, scores, 0.0), axis=0, keepdims=True))
        idxs.append(sel)
        val = jnp.where(hit, -jnp.inf, val)
    wk = jnp.concatenate(ws, axis=0)
    idx_ref[...] = jnp.concatenate(idxs, axis=0).astype(I32)
    w_ref[...] = wk / jnp.sum(wk, axis=0, keepdims=True) * ROUTED_SCALE

    a = _silu(_dot(hh, wsg_ref[...])) * _dot(hh, wsu_ref[...])
    base_ref[...] = x1 + _dot(a.astype(BF16), wsd_ref[...])

    bits = lax.bitcast_convert_type(hh.astype(F32), U32)
    lo = lax.shift_right_logical(bits[:, :d // 2], jnp.uint32(16))
    hi = bits[:, d // 2:] & jnp.uint32(0xFFFF0000)
    h2u_ref[...] = hi | lo


def _ffn_prep(x1, w, tm):
    t, d = x1.shape
    ds = w["w_s_gate"].shape[1]
    row = lambda i: (i, 0)
    col = lambda i: (0, i)
    return pl.pallas_call(
        _ffn_prep_kernel,
        grid=(t // tm,),
        in_specs=[pl.BlockSpec((tm, d), row), _const_spec((1, d)), _const_spec((N_EXPERTS, d)),
                  _const_spec((N_EXPERTS, d)), _const_spec((N_EXPERTS, 1)), _const_spec((d, ds)),
                  _const_spec((d, ds)), _const_spec((ds, d))],
        out_specs=[pl.BlockSpec((tm, d), row), pl.BlockSpec((tm, d // 2), row),
                   pl.BlockSpec((TOP_K, tm), col), pl.BlockSpec((TOP_K, tm), col)],
        out_shape=[jax.ShapeDtypeStruct((t, d), F32), jax.ShapeDtypeStruct((t, d // 2), U32),
                   jax.ShapeDtypeStruct((TOP_K, t), I32), jax.ShapeDtypeStruct((TOP_K, t), F32)],
        compiler_params=_params(("parallel",), 40),
        name="ffn_prep",
    )(x1, w["g_ffn"], w["w_r_hi"], w["w_r_lo"], w["b_router"], w["w_s_gate"], w["w_s_up"], w["w_s_down"])


def _route_kernel(idx_ref, tri_ref, low_ref, dest_ref, pos_ref, rstart_ref, rcnt_ref, blk_ref, nvalid_ref,
                  cnt_scr, run_scr, *, blk):
    p = pl.program_id(0)
    i = pl.program_id(1)
    tm = idx_ref.shape[1]
    nsub = tm // ROUTE_SUB
    idx = idx_ref[...]
    eio = lax.broadcasted_iota(I32, (N_EXPERTS, ROUTE_SUB), 0)
    onehots, run_rows = [], []
    for s in range(nsub):
        ids = idx[:, s * ROUTE_SUB:(s + 1) * ROUTE_SUB]
        oh = jnp.zeros((N_EXPERTS, ROUTE_SUB), F32)
        for k in range(TOP_K):
            oh = oh + (eio == ids[k:k + 1, :]).astype(F32)
        onehots.append(oh)
        cnt_s = jnp.sum(oh, axis=1, keepdims=True)
        run_rows.append(jnp.ceil(cnt_s * (1.0 / RUN_ALIGN)) * RUN_ALIGN)

    @pl.when((p == 0) & (i == 0))
    def _():
        cnt_scr[...] = jnp.zeros(cnt_scr.shape, F32)

    @pl.when(p == 0)
    def _():
        cnt_scr[...] += sum(run_rows)

    @pl.when((p == 1) & (i == 0))
    def _():
        cnt = cnt_scr[...]
        padded = jnp.ceil(cnt * (1.0 / blk)) * blk
        r = lax.broadcasted_iota(I32, (N_EXPERTS, N_EXPERTS), 0)
        c = lax.broadcasted_iota(I32, (N_EXPERTS, N_EXPERTS), 1)
        pb = jnp.broadcast_to(padded, (N_EXPERTS, N_EXPERTS))
        prow = jnp.sum(jnp.where(r == c, pb, 0.0), axis=0, keepdims=True)
        prb = jnp.broadcast_to(prow, (N_EXPERTS, N_EXPERTS))
        pstart = jnp.sum(jnp.where(c < r, prb, 0.0), axis=1, keepdims=True)
        run_scr[...] = pstart
        pend = pstart + padded
        nb = blk_ref.shape[1]
        bstart = (lax.broadcasted_iota(I32, (N_EXPERTS, nb), 1) * blk).astype(F32)
        pend_b = jnp.broadcast_to(pend, (N_EXPERTS, nb))
        pstart_b = jnp.broadcast_to(pstart, (N_EXPERTS, nb))
        be = jnp.sum((pend_b <= bstart).astype(F32), axis=0, keepdims=True)
        blk_ref[...] = jnp.minimum(be, N_EXPERTS - 1.0).astype(I32)
        inside = (pstart_b <= bstart) & (bstart < pend_b)
        rows = jnp.clip(jnp.broadcast_to(cnt, (N_EXPERTS, nb)) - (bstart - pstart_b), 0.0, float(blk))
        nvalid_ref[...] = jnp.sum(jnp.where(inside, rows, 0.0), axis=0, keepdims=True).astype(I32)

    @pl.when(p == 1)
    def _():
        lane = lax.broadcasted_iota(I32, (N_EXPERTS, LANES), 1)
        rstart = jnp.zeros((N_EXPERTS, LANES), F32)
        rcnt = jnp.zeros((N_EXPERTS, LANES), F32)
        start = run_scr[...]
        for s in range(nsub):
            ids = idx[:, s * ROUTE_SUB:(s + 1) * ROUTE_SUB]
            within = _dot(onehots[s].astype(BF16), tri_ref[...])
            rows_b = jnp.broadcast_to(run_rows[s], (N_EXPERTS, ROUTE_SUB)).astype(BF16)
            local = _dot(low_ref[...], rows_b)
            d_rows, p_rows = [], []
            for k in range(TOP_K):
                hit = eio == ids[k:k + 1, :]
                d_rows.append(jnp.sum(jnp.where(hit, start + within, 0.0), axis=0, keepdims=True))
                p_rows.append(jnp.sum(jnp.where(hit, local + within, 0.0), axis=0, keepdims=True))
            sl = slice(s * ROUTE_SUB, (s + 1) * ROUTE_SUB)
            dest_ref[:, sl] = jnp.concatenate(d_rows, axis=0).astype(I32)
            pos_ref[:, sl] = jnp.concatenate(p_rows, axis=0).astype(I32)
            rstart = jnp.where(lane == s, start, rstart)
            rcnt = jnp.where(lane == s, run_rows[s], rcnt)
            start = start + run_rows[s]
        rstart_ref[0] = rstart.astype(I32)
        rcnt_ref[0] = rcnt.astype(I32)
        run_scr[...] = start


def _route(idx_pad, blk, nb_pad):
    tpad = idx_pad.shape[1]
    tm = ROUTE_TILE
    nsteps = tpad // tm
    sub = ROUTE_SUB
    tri = (lax.broadcasted_iota(I32, (sub, sub), 0) < lax.broadcasted_iota(I32, (sub, sub), 1)).astype(BF16)
    low = (lax.broadcasted_iota(I32, (N_EXPERTS, N_EXPERTS), 1)
           < lax.broadcasted_iota(I32, (N_EXPERTS, N_EXPERTS), 0)).astype(BF16)
    tok = lambda p, i: (0, i * p)
    run = lambda p, i: (i * p, 0, 0)
    return pl.pallas_call(
        functools.partial(_route_kernel, blk=blk),
        grid=(2, nsteps),
        in_specs=[pl.BlockSpec((TOP_K, tm), lambda p, i: (0, i)), _const_spec((sub, sub)),
                  _const_spec((N_EXPERTS, N_EXPERTS))],
        out_specs=[pl.BlockSpec((TOP_K, tm), tok), pl.BlockSpec((TOP_K, tm), tok),
                   pl.BlockSpec((1, N_EXPERTS, LANES), run), pl.BlockSpec((1, N_EXPERTS, LANES), run),
                   _const_spec((1, nb_pad)), _const_spec((1, nb_pad))],
        out_shape=[jax.ShapeDtypeStruct((TOP_K, tpad), I32), jax.ShapeDtypeStruct((TOP_K, tpad), I32),
                   jax.ShapeDtypeStruct((nsteps, N_EXPERTS, LANES), I32),
                   jax.ShapeDtypeStruct((nsteps, N_EXPERTS, LANES), I32),
                   jax.ShapeDtypeStruct((1, nb_pad), I32), jax.ShapeDtypeStruct((1, nb_pad), I32)],
        scratch_shapes=[pltpu.VMEM((N_EXPERTS, 1), F32), pltpu.VMEM((N_EXPERTS, 1), F32)],
        compiler_params=_params(("arbitrary", "arbitrary"), 32),
        name="moe_route",
    )(idx_pad, tri, low)


def _dispatch_kernel(dest_ref, h2u_ref, xs_in_ref, xs_ref, sem):
    del xs_in_ref
    tm = dest_ref.shape[1]

    def row_copy(t, k):
        return pltpu.make_async_copy(h2u_ref.at[pl.ds(t, 1)], xs_ref.at[pl.ds(dest_ref[k, t], 1)], sem)

    def body(t, carry):
        for k in range(TOP_K):
            row_copy(t, k).start(priority=k % 2)
        return carry

    lax.fori_loop(0, tm, body, 0)

    def drain(t, carry):
        for k in range(TOP_K):
            row_copy(t, k).wait()
        return carry

    lax.fori_loop(0, tm, drain, 0)


def _dispatch(dest, h2u, xs, tm):
    t, half = h2u.shape
    return pl.pallas_call(
        _dispatch_kernel,
        grid=(t // tm,),
        in_specs=[pl.BlockSpec((TOP_K, tm), lambda i: (0, i), memory_space=pltpu.SMEM),
                  pl.BlockSpec((tm, half), lambda i: (i, 0)), pl.BlockSpec(memory_space=pl.ANY)],
        out_specs=pl.BlockSpec(memory_space=pl.ANY),
        out_shape=jax.ShapeDtypeStruct(xs.shape, xs.dtype),
        scratch_shapes=[pltpu.SemaphoreType.DMA(())],
        input_output_aliases={2: 0},
        compiler_params=pltpu.CompilerParams(dimension_semantics=("arbitrary",), has_side_effects=True),
        name="moe_dispatch",
    )(dest, h2u, xs)


def _unpack_pairs(word):
    lo = lax.bitcast_convert_type(lax.shift_left(word, jnp.uint32(16)), F32).astype(BF16)
    hi = lax.bitcast_convert_type(word & jnp.uint32(0xFFFF0000), F32).astype(BF16)
    return lo, hi


def _pack_pairs(x):
    half = x.shape[1] // 2
    bits = lax.bitcast_convert_type(x.astype(BF16).astype(F32), U32)
    return (bits[:, half:] & jnp.uint32(0xFFFF0000)) | lax.shift_right_logical(bits[:, :half], jnp.uint32(16))


def _expert_kernel(be_ref, bm_ref, xs_ref, wg_ref, wu_ref, wd_ref, y_ref):
    b = pl.program_id(0)
    half = xs_ref.shape[1]

    @pl.when(bm_ref[b] == b)
    def _():
        lo, hi = _unpack_pairs(xs_ref[...])
        g = _dot(lo, wg_ref[0, :half, :]) + _dot(hi, wg_ref[0, half:, :])
        u = _dot(lo, wu_ref[0, :half, :]) + _dot(hi, wu_ref[0, half:, :])
        y_ref[...] = _pack_pairs(_dot((_silu(g) * u).astype(BF16), wd_ref[0]))


def _experts(block_e, block_map, xs, wg, wu, wd, blk):
    nslots, half = xs.shape
    d = 2 * half
    de = wg.shape[2]
    grid_spec = pltpu.PrefetchScalarGridSpec(
        num_scalar_prefetch=2,
        grid=(nslots // blk,),
        in_specs=[pl.BlockSpec((blk, half), lambda b, be, bm: (bm[b], 0)),
                  pl.BlockSpec((1, d, de), lambda b, be, bm: (be[bm[b]], 0, 0)),
                  pl.BlockSpec((1, d, de), lambda b, be, bm: (be[bm[b]], 0, 0)),
                  pl.BlockSpec((1, de, d), lambda b, be, bm: (be[bm[b]], 0, 0))],
        out_specs=pl.BlockSpec((blk, half), lambda b, be, bm: (bm[b], 0)),
    )
    return pl.pallas_call(
        _expert_kernel,
        grid_spec=grid_spec,
        out_shape=jax.ShapeDtypeStruct((nslots, half), U32),
        compiler_params=_params(("arbitrary",), 32),
        name="moe_experts",
    )(block_e, block_map, xs, wg, wu, wd)


def _combine_kernel(rs_ref, rc_ref, pos_ref, w_ref, base_ref, y_ref, out_ref, ybuf, sem, *, tile0):
    i = pl.program_id(0)
    n = pl.num_programs(0)
    tm, d = base_ref.shape
    half = d // 2
    nrow = ybuf.shape[1]

    def run_copy(tile, slot, e, off):
        rows = pl.multiple_of(rc_ref[tile * N_EXPERTS + e], RUN_ALIGN)
        src = y_ref.at[pl.ds(pl.multiple_of(rs_ref[tile * N_EXPERTS + e], RUN_ALIGN), rows)]
        dst = ybuf.at[slot, pl.ds(pl.multiple_of(off, RUN_ALIGN), rows)]
        return rows, pltpu.make_async_copy(src, dst, sem.at[slot])

    def for_runs(tile, slot, act):
        def body(e, off):
            rows, cp = run_copy(tile, slot, e, off)

            @pl.when(rows > 0)
            def _():
                act(cp)
            return off + rows
        lax.fori_loop(0, N_EXPERTS, body, 0)

    @pl.when(i == 0)
    def _():
        ybuf[...] = jnp.zeros(ybuf.shape, U32)
        for_runs(tile0, 0, lambda cp: cp.start())

    @pl.when(i + 1 < n)
    def _():
        for_runs(tile0 + i + 1, (i + 1) % 2, lambda cp: cp.start())

    slot = i % 2
    for_runs(tile0 + i, slot, lambda cp: cp.wait())

    pos = pos_ref[...]
    w = w_ref[...]
    col = lax.broadcasted_iota(I32, (tm, nrow), 1)
    pmat = jnp.zeros((tm, nrow), F32)
    for k in range(TOP_K):
        pmat = pmat + jnp.where(col == pos[:, k:k + 1], w[:, k:k + 1], 0.0)
    p_hi, p_lo = _split(pmat)
    y_lo, y_hi = _unpack_pairs(ybuf[slot])
    base = base_ref[...]
    out_ref[:, :half] = base[:, :half] + _dot(p_hi, y_lo) + _dot(p_lo, y_lo)
    out_ref[:, half:] = base[:, half:] + _dot(p_hi, y_hi) + _dot(p_lo, y_hi)


def _combine(run_start, run_rows, pos_t, w_t, base, y, tile0):
    t, d = base.shape
    tm = ROUTE_SUB
    nrow = -(-(TOP_K * tm + N_EXPERTS * (RUN_ALIGN - 1)) // LANES) * LANES
    grid_spec = pltpu.PrefetchScalarGridSpec(
        num_scalar_prefetch=2,
        grid=(t // tm,),
        in_specs=[pl.BlockSpec((tm, TOP_K), lambda i, rs, rc: (i, 0)),
                  pl.BlockSpec((tm, TOP_K), lambda i, rs, rc: (i, 0)),
                  pl.BlockSpec((tm, d), lambda i, rs, rc: (i, 0)),
                  pl.BlockSpec(memory_space=pl.ANY)],
        out_specs=pl.BlockSpec((tm, d), lambda i, rs, rc: (i, 0)),
        scratch_shapes=[pltpu.VMEM((2, nrow, d // 2), U32), pltpu.SemaphoreType.DMA((2,))],
    )
    return pl.pallas_call(
        functools.partial(_combine_kernel, tile0=tile0),
        grid_spec=grid_spec,
        out_shape=jax.ShapeDtypeStruct((t, d), F32),
        compiler_params=_params(("arbitrary",), 40),
        name="moe_combine",
    )(run_start, run_rows, pos_t, w_t, base, y)


def _sample_in_kernel(x_ref, ga_ref, wa_ref, gq_ref, wuq_ref, gqm_ref, gkv_ref, wuk_ref, e32_ref, gkm_ref,
                      gkn_ref, cos_ref, sa_ref, sb_ref, wb_ref, st_ref, wdw_ref, bdw_ref, gln_ref, bln_ref,
                      wco_ref, gqmem_ref,
                      c_ref, kr_ref, u_ref, qmn_ref, gsig_ref, yconv_ref, qabs_ref, qrope_ref, snew_ref):
    h, q_raw, k_raw, c, _, kr = _qkv_core(x_ref[...], ga_ref[...], wa_ref[...], gq_ref[...], wuq_ref[...],
                                          gqm_ref[...], gkv_ref[...], wuk_ref[...], e32_ref[...], gkm_ref[...])
    c_ref[...] = c
    kr_ref[...] = kr
    cos, sa, sb = cos_ref[...], sa_ref[...], sb_ref[...]
    for hd in range(MLA_HEADS):
        sl = slice(hd * HEAD_PAD, (hd + 1) * HEAD_PAD)
        qh = _rope128(_head_norm(q_raw[:, sl], gqm_ref[:, sl]), cos, sa, sb)
        kh = _rope128(_head_norm(k_raw[:, sl], gkm_ref[:, sl]), cos, sa, sb)
        snew_ref[:, hd:hd + 1] = jnp.sum(qh * kh, axis=-1, keepdims=True) * MLA_SCALE
        qrope_ref[hd] = qh[:, QK_NOPE:QK_HEAD]
        g_hi, g_lo = _split(qh * gkn_ref[:, sl])
        wk = wuk_ref[:, sl]
        qabs_ref[hd] = _dot_nt(g_hi, wk) + _dot_nt(g_lo, wk)

    pb = _dot(h, wb_ref[...])
    u = pb[:, :C_CONV] * _sigmoid(pb[:, C_CONV:2 * C_CONV])
    u_ref[...] = u
    for hd in range(MEM_HEADS):
        sl = slice(2 * C_CONV + hd * MEM_HD, 2 * C_CONV + (hd + 1) * MEM_HD)
        qmn_ref[:, hd * MEM_HD:(hd + 1) * MEM_HD] = _rms(pb[:, sl], gqmem_ref[...])
    gsig_ref[...] = _sigmoid(pb[:, 2 * C_CONV + MEM_W:])

    y = bdw_ref[...] + wdw_ref[CONV_W - 1:CONV_W, :] * u
    for tap in range(CONV_W - 1):
        y = y + wdw_ref[tap:tap + 1, :] * st_ref[tap]
    yconv_ref[...] = _dot(_ln_silu(y, gln_ref[...], bln_ref[...]).astype(BF16), wco_ref[...])


def _sample_in(xs2d, state_t, w, tabs):
    db, d = xs2d.shape
    ins = [xs2d, w["g_attn"], w["w_a"], w["g_q_lora"], w["w_uq_p"], w["g_q_mla_p"], w["g_kv_lora"], w["w_uk_p"],
           w["e32"], w["g_k_mla_p"], w["g_k_nope_p"], *tabs, w["w_b"], state_t, w["w_dw"], w["b_dw"],
           w["g_conv_ln"], w["b_conv_ln"], w["w_conv_out"], w["g_q_mem"]]
    outs = [jax.ShapeDtypeStruct((db, KV_LORA), F32), jax.ShapeDtypeStruct((db, QK_ROPE), F32),
            jax.ShapeDtypeStruct((db, C_CONV), F32), jax.ShapeDtypeStruct((db, MEM_W), F32),
            jax.ShapeDtypeStruct((db, N_BRANCH * d), F32), jax.ShapeDtypeStruct((db, d), F32),
            jax.ShapeDtypeStruct((MLA_HEADS, db, KV_LORA), F32), jax.ShapeDtypeStruct((MLA_HEADS, db, QK_ROPE), F32),
            jax.ShapeDtypeStruct((db, MLA_HEADS), F32)]
    return pl.pallas_call(
        _sample_in_kernel,
        grid=(1,),
        in_specs=[_const_spec(a.shape) for a in ins],
        out_specs=[_const_spec(o.shape) for o in outs],
        out_shape=outs,
        compiler_params=_params(("arbitrary",), 56),
        name="sample_in",
    )(*ins)


def _sample_mla_kernel(pt_ref, qabs_ref, qrope_ref, wukt_ref, g1_ref, g2_ref, cost_ref, sint_ref,
                       lat_ref, kro_ref, ctx_ref, m_ref, l_ref, cbuf, krbuf, lhs, sem_c, sem_k, *, n_chunks, ch):
    b = pl.program_id(0)
    nseq = pl.num_programs(0)
    page = cbuf.shape[2]
    pc = ch * page
    nrow = MLA_HEADS * QK_NOPE

    def page_copies(seq, chunk, slot, p):
        pid = pt_ref[seq, chunk * ch + p]
        return (pltpu.make_async_copy(lat_ref.at[pid], cbuf.at[slot, p], sem_c.at[slot]),
                pltpu.make_async_copy(kro_ref.at[pid], krbuf.at[slot, p], sem_k.at[slot]))

    def issue(seq, chunk, slot):
        for p in range(ch):
            a, k = page_copies(seq, chunk, slot, p)
            a.start()
            k.start()

    @pl.when(b == 0)
    def _():
        lhs[0:nrow, :] = wukt_ref[...]
        issue(0, 0, 0)

    qa = jnp.concatenate([qabs_ref[0], jnp.zeros((16 - MLA_HEADS, KV_LORA), F32)], axis=0)
    lhs[nrow:nrow + 16, :] = qa.astype(BF16)
    qr = qrope_ref[0].astype(BF16)
    g1 = g1_ref[...]
    g2 = g2_ref[...]

    def chunk_body(c, carry):
        m_prev, l_prev, ctx = carry
        slot = c % 2

        @pl.when(c + 1 < n_chunks)
        def _():
            issue(b, c + 1, 1 - slot)

        @pl.when((c + 1 == n_chunks) & (b + 1 < nseq))
        def _():
            issue(b + 1, 0, 1 - slot)

        for p in range(ch):
            a, k = page_copies(b, c, slot, p)
            a.wait()
            k.wait()

        cb = cbuf[slot].reshape(pc, KV_LORA).astype(BF16)
        kt = _dot_nt(lhs[...], cb)
        ssn = jnp.concatenate(
            [jnp.sum(jnp.square(kt[h * QK_NOPE:(h + 1) * QK_NOPE]), axis=0, keepdims=True)
             for h in range(MLA_HEADS)], axis=0)
        s_nope = kt[nrow:nrow + MLA_HEADS]

        krt = jnp.concatenate([krbuf[slot, p] for p in range(ch)], axis=1)
        ssr = jnp.sum(krt * krt, axis=0, keepdims=True)
        half = QK_ROPE // 2
        rot = jnp.concatenate([-krt[half:], krt[:half]], axis=0)
        roped = krt * g1 * cost_ref[c] + rot * g2 * sint_ref[c]
        s_rope = _dot(qr, roped.astype(BF16))

        rinv = lax.rsqrt((ssn + ssr) * (1.0 / QK_HEAD) + EPS)
        s = (s_nope + s_rope) * rinv * MLA_SCALE
        m_new = jnp.maximum(m_prev, jnp.max(s, axis=-1, keepdims=True))
        alpha = jnp.exp(m_prev - m_new)
        pexp = jnp.exp(s - m_new)
        l_new = alpha * l_prev + jnp.sum(pexp, axis=-1, keepdims=True)
        ctx_new = alpha * ctx + _dot(pexp.astype(BF16), cb)
        return m_new, l_new, ctx_new

    init = (jnp.full((MLA_HEADS, 1), NEG_BIG, F32), jnp.zeros((MLA_HEADS, 1), F32),
            jnp.zeros((MLA_HEADS, KV_LORA), F32))
    m_fin, l_fin, ctx = lax.fori_loop(0, n_chunks, chunk_body, init)
    ctx_ref[0] = ctx
    m_ref[0] = jnp.broadcast_to(m_fin, (MLA_HEADS, LANES))
    l_ref[0] = jnp.broadcast_to(l_fin, (MLA_HEADS, LANES))


def _sample_mla(page_table, qabs, qrope, lat, kro, w, ch):
    db, n_pages = page_table.shape
    page = lat.shape[1]
    n_chunks = n_pages // ch
    pc = ch * page
    nrow = MLA_HEADS * QK_NOPE
    grid_spec = pltpu.PrefetchScalarGridSpec(
        num_scalar_prefetch=1,
        grid=(db,),
        in_specs=[pl.BlockSpec((1, MLA_HEADS, KV_LORA), lambda b, pt: (b, 0, 0)),
                  pl.BlockSpec((1, MLA_HEADS, QK_ROPE), lambda b, pt: (b, 0, 0)),
                  pl.BlockSpec((nrow, KV_LORA), lambda b, pt: (0, 0)),
                  pl.BlockSpec((QK_ROPE, 1), lambda b, pt: (0, 0)),
                  pl.BlockSpec((QK_ROPE, 1), lambda b, pt: (0, 0)),
                  pl.BlockSpec((n_chunks, QK_ROPE, pc), lambda b, pt: (0, 0, 0)),
                  pl.BlockSpec((n_chunks, QK_ROPE, pc), lambda b, pt: (0, 0, 0)),
                  pl.BlockSpec(memory_space=pl.ANY), pl.BlockSpec(memory_space=pl.ANY)],
        out_specs=[pl.BlockSpec((1, MLA_HEADS, KV_LORA), lambda b, pt: (b, 0, 0)),
                   pl.BlockSpec((1, MLA_HEADS, LANES), lambda b, pt: (b, 0, 0)),
                   pl.BlockSpec((1, MLA_HEADS, LANES), lambda b, pt: (b, 0, 0))],
        scratch_shapes=[pltpu.VMEM((2, ch, page, KV_LORA), F32), pltpu.VMEM((2, ch, QK_ROPE, page), F32),
                        pltpu.VMEM((nrow + 16, KV_LORA), BF16),
                        pltpu.SemaphoreType.DMA((2,)), pltpu.SemaphoreType.DMA((2,))],
    )
    return pl.pallas_call(
        functools.partial(_sample_mla_kernel, n_chunks=n_chunks, ch=ch),
        grid_spec=grid_spec,
        out_shape=[jax.ShapeDtypeStruct((db, MLA_HEADS, KV_LORA), F32),
                   jax.ShapeDtypeStruct((db, MLA_HEADS, LANES), F32),
                   jax.ShapeDtypeStruct((db, MLA_HEADS, LANES), F32)],
        compiler_params=_params(("arbitrary",), 56),
        name="sample_mla",
    )(page_table, qabs, qrope, w["w_uk_t"], w["g_kr1"], w["g_kr2"], w["cos_t"], w["sin_t"],
      lat, kro)


def _sample_mem_kernel(q_ref, k_ref, v_ref, o_ref):
    q = q_ref[0]
    s = jnp.sum(k_ref[0] * q, axis=-1, keepdims=True) * MEM_SCALE
    e = jnp.exp(s - jnp.max(s, axis=0, keepdims=True))
    p = e / jnp.sum(e, axis=0, keepdims=True)
    o_ref[0] = jnp.sum(p * v_ref[0], axis=0)


def _sample_mem(qmn, mk, mv):
    db, mtok = mk.shape[:2]
    q3 = qmn.reshape(db, MEM_HEADS, MEM_HD)
    kv_spec = pl.BlockSpec((1, mtok, MEM_HEADS, MEM_HD), lambda b: (b, 0, 0, 0))
    q_spec = pl.BlockSpec((1, MEM_HEADS, MEM_HD), lambda b: (b, 0, 0))
    return pl.pallas_call(
        _sample_mem_kernel,
        grid=(db,),
        in_specs=[q_spec, kv_spec, kv_spec],
        out_specs=q_spec,
        out_shape=jax.ShapeDtypeStruct((db, MEM_HEADS, MEM_HD), F32),
        compiler_params=_params(("parallel",), 32),
        name="sample_mem",
    )(q3, mk, mv).reshape(db, MEM_W)


def _sample_out_kernel(x_ref, ctx_ref, m_ref, l_ref, snew_ref, c_ref, wuv_ref, womla_ref, omem_ref, wom_ref,
                       gsig_ref, yconv_ref, wout_ref, x1_ref):
    c_new = c_ref[...]
    m = m_ref[...]
    l = l_ref[...]
    s_new = snew_ref[...]
    d = x_ref.shape[-1]
    y_mla = jnp.zeros((x_ref.shape[0], d), F32)
    for h in range(MLA_HEADS):
        mh, lh, sh = m[:, h:h + 1], l[:, h:h + 1], s_new[:, h:h + 1]
        m_fin = jnp.maximum(mh, sh)
        a_old = jnp.exp(mh - m_fin)
        a_new = jnp.exp(sh - m_fin)
        denom = lh * a_old + a_new
        ctx = (ctx_ref[h] * a_old + a_new * c_new) / denom
        o_h = _dot(ctx.astype(BF16), wuv_ref[h])
        y_mla = y_mla + _dot(o_h.astype(BF16), womla_ref[h])
    y_mem = _dot(omem_ref[...].astype(BF16), wom_ref[...])
    x1_ref[...] = _merge_out(x_ref[...], gsig_ref[...], y_mla, yconv_ref[...], y_mem, wout_ref[...])


def _sample_out(xs2d, ctx_t, m, l, s_new, c_s, o_mem, gsig, y_conv, w):
    ins = [xs2d, ctx_t, m, l, s_new, c_s, w["w_uv_h"], w["w_o_mla_h"], o_mem, w["w_o_mem"], gsig, y_conv,
           w["w_out"]]
    return pl.pallas_call(
        _sample_out_kernel,
        grid=(1,),
        in_specs=[_const_spec(a.shape) for a in ins],
        out_specs=_const_spec(xs2d.shape),
        out_shape=jax.ShapeDtypeStruct(xs2d.shape, F32),
        compiler_params=_params(("arbitrary",), 40),
        name="sample_out",
    )(*ins)


def _prep_weights(g_attn_norm, w_in, g_q_lora, w_uq, g_q_mla, g_kv_lora, w_uk, w_uv, g_k_mla, w_o_mla, w_dw, b_dw,
                  g_conv_ln, b_conv_ln, w_conv_out, g_mem_norm, w_mem_kv, g_q_mem, g_k_mem, w_o_mem, w_out,
                  g_ffn_norm, w_router, b_router, w_e_gate, w_e_up, w_e_down, w_s_gate, w_s_up, w_s_down):
    row = lambda g: g.reshape(1, -1).astype(F32)
    e32 = jnp.zeros((QK_ROPE, MLA_HEADS, HEAD_PAD), F32)
    e32 = e32.at[jnp.arange(QK_ROPE), :, QK_NOPE + jnp.arange(QK_ROPE)].set(1.0)
    half = QK_ROPE // 2
    g_rope = g_k_mla[QK_NOPE:]
    w_r_hi = w_router.T.astype(BF16)
    w_r_lo = (w_router.T - w_r_hi.astype(F32)).astype(BF16)
    g_k_nope = jnp.concatenate([g_k_mla[:QK_NOPE], jnp.zeros((QK_ROPE,), F32)])
    return {
        "g_attn": row(g_attn_norm),
        "w_a": w_in[:, :QKV_COLS].astype(BF16),
        "w_b": w_in[:, QKV_COLS:].astype(BF16),
        "g_q_lora": row(g_q_lora),
        "w_uq_p": _pad_heads(w_uq, QK_HEAD).astype(BF16),
        "g_q_mla_p": _pad_heads(jnp.tile(g_q_mla, MLA_HEADS).reshape(1, -1), QK_HEAD),
        "g_kv_lora": row(g_kv_lora),
        "w_uk_p": _pad_heads(w_uk.reshape(KV_LORA, MLA_HEADS * QK_NOPE), QK_NOPE).astype(BF16),
        "e32": e32.reshape(QK_ROPE, MLA_HEADS * HEAD_PAD).astype(BF16),
        "g_k_mla_p": _pad_heads(jnp.tile(g_k_mla, MLA_HEADS).reshape(1, -1), QK_HEAD),
        "g_k_nope_p": _pad_heads(jnp.tile(g_k_nope, MLA_HEADS).reshape(1, -1), QK_HEAD),
        "w_uv_p": _pad_heads(w_uv.reshape(KV_LORA, MLA_HEADS * V_HEAD), V_HEAD).astype(BF16),
        "w_uv_h": w_uv.transpose(1, 0, 2).astype(BF16),
        "w_uk_t": w_uk.reshape(KV_LORA, MLA_HEADS * QK_NOPE).T.astype(BF16),
        "g_kr1": g_rope.reshape(QK_ROPE, 1),
        "g_kr2": jnp.concatenate([g_rope[half:], g_rope[:half]]).reshape(QK_ROPE, 1),
        "w_o_mla": w_o_mla.astype(BF16),
        "w_o_mla_h": w_o_mla.reshape(MLA_HEADS, V_HEAD, -1).astype(BF16),
        "w_dw": w_dw.astype(F32), "b_dw": row(b_dw), "g_conv_ln": row(g_conv_ln), "b_conv_ln": row(b_conv_ln),
        "w_conv_out": w_conv_out.astype(BF16),
        "g_mem_norm": row(g_mem_norm), "w_mem_kv": w_mem_kv.astype(BF16),
        "g_q_mem": row(g_q_mem), "g_k_mem": row(g_k_mem),
        "w_o_mem": w_o_mem.astype(BF16), "w_out": w_out.astype(BF16),
        "g_ffn": row(g_ffn_norm), "w_r_hi": w_r_hi, "w_r_lo": w_r_lo,
        "b_router": b_router.reshape(N_EXPERTS, 1).astype(F32),
        "w_e_gate": w_e_gate.astype(BF16), "w_e_up": w_e_up.astype(BF16), "w_e_down": w_e_down.astype(BF16),
        "w_s_gate": w_s_gate.astype(BF16), "w_s_up": w_s_up.astype(BF16), "w_s_down": w_s_down.astype(BF16),
    }


def _layer(xp, xs, mem_prompt, cache_latent, cache_krope, state_conv, cache_mem_k, cache_mem_v, page_table, w):
    b, s, d = xp.shape
    db, ds, _ = xs.shape
    assert ds == 1, "the sample group decodes one token per sequence"
    n_pages, page = page_table.shape[1], cache_latent.shape[1]
    past = n_pages * page
    mtok = mem_prompt.shape[1]
    xp2 = xp.reshape(b * s, d)
    xs2 = xs.reshape(db * ds, d)

    mk_p, mv_p = _mem_kv(mem_prompt.reshape(b * mtok, d), w["g_mem_norm"], w["w_mem_kv"], w["g_k_mem"])
    q, k, v, c_p, kr_p = _qkv(xp2, s, w, _rope_tables(jnp.arange(s)))
    o_attn = _flash(q, k, v, b, s)
    x1_p, conv_p = _mixers(xp2, o_attn, mk_p.reshape(b, mtok, MEM_W), mv_p.reshape(b, mtok, MEM_W), b, s, w)

    ch = math.gcd(n_pages, 16)
    pc = ch * page
    half = QK_ROPE // 2
    inv = ROPE_THETA ** (-jnp.arange(half, dtype=F32) / half)
    ang = inv[:, None] * jnp.arange(past, dtype=F32)[None, :]
    tab = lambda a: jnp.concatenate([a, a], axis=0).reshape(QK_ROPE, past // pc, pc).transpose(1, 0, 2)
    ws = dict(w, cos_t=tab(jnp.cos(ang)), sin_t=tab(jnp.sin(ang)))
    tabs_s = _rope_tables(jnp.full((db,), past, I32))
    (c_s, kr_s, u_s, qmn, gsig_s, yconv_s, qabs, qrope, s_new) = _sample_in(
        xs2, state_conv.transpose(1, 0, 2), w, tabs_s)
    ctx, m_s, l_s = _sample_mla(page_table, qabs.transpose(1, 0, 2), qrope.transpose(1, 0, 2),
                                cache_latent, cache_krope.transpose(0, 2, 1), ws, ch)
    o_mem = _sample_mem(qmn, cache_mem_k, cache_mem_v)
    x1_s = _sample_out(xs2, ctx.transpose(1, 0, 2), m_s[:, :, 0], l_s[:, :, 0], s_new, c_s, o_mem, gsig_s,
                       yconv_s, w)

    base_p, h2u_p, idx_p, wk_p = _ffn_prep(x1_p, w, min(512, b * s))
    base_s, h2u_s, idx_s, wk_s = _ffn_prep(x1_s, w, db)
    tp, tsm = b * s, db
    t_all = tp + tsm
    assert tp % ROUTE_SUB == 0 and tsm % ROUTE_SUB == 0, "token groups must fill whole routing tiles"
    blk = 512
    t_pad = -(-t_all // ROUTE_TILE) * ROUTE_TILE
    idx_all = jnp.concatenate([idx_p, idx_s, jnp.full((TOP_K, t_pad - t_all), N_EXPERTS, I32)], axis=1)
    n_tiles = t_all // ROUTE_SUB
    max_rows = t_all * TOP_K + n_tiles * N_EXPERTS * (RUN_ALIGN - 1) + N_EXPERTS * (blk - 1)
    n_blocks = -(-max_rows // blk)
    nb_pad = -(-n_blocks // LANES) * LANES
    dest, pos, run_start, run_rows, block_e, nvalid = _route(idx_all, blk, nb_pad)
    per_step = ROUTE_TILE // ROUTE_SUB
    flat = lambda a: a[:, :, :per_step].transpose(0, 2, 1).reshape(-1)
    run_start, run_rows = flat(run_start), flat(run_rows)
    n_used = jnp.sum((nvalid[0, :n_blocks] > 0).astype(I32))
    block_map = jnp.minimum(jnp.arange(n_blocks, dtype=I32), n_used - 1)
    xs_sorted = jnp.zeros((n_blocks * blk, d // 2), U32)
    xs_sorted = _dispatch(dest[:, :tp], h2u_p, xs_sorted, min(256, tp))
    xs_sorted = _dispatch(dest[:, tp:t_all], h2u_s, xs_sorted, tsm)
    y = _experts(block_e[0, :n_blocks], block_map, xs_sorted, w["w_e_gate"], w["w_e_up"], w["w_e_down"], blk)
    pos_t = pos.T
    yp = _combine(run_start, run_rows, pos_t[:tp], wk_p.T, base_p, y, 0)
    ys = _combine(run_start, run_rows, pos_t[tp:t_all], wk_s.T, base_s, y, tp // ROUTE_SUB)

    new_conv_s = jnp.concatenate([state_conv[:, 1:], u_s[:, None, :]], axis=1)
    return (yp.reshape(b, s, d), ys.reshape(db, ds, d), c_p.reshape(b, s, KV_LORA), kr_p.reshape(b, s, QK_ROPE),
            conv_p, mk_p.reshape(b, mtok, MEM_HEADS, MEM_HD), mv_p.reshape(b, mtok, MEM_HEADS, MEM_HD),
            c_s.reshape(db, ds, KV_LORA), kr_s.reshape(db, ds, QK_ROPE), new_conv_s)


def kernel(x_prompt, x_sample, mem_prompt, cache_latent, cache_krope, state_conv, cache_mem_k, cache_mem_v,
           page_table, g_attn_norm, w_in, g_q_lora, w_uq, g_q_mla, g_kv_lora, w_uk, w_uv, g_k_mla, w_o_mla, w_dw,
           b_dw, g_conv_ln, b_conv_ln, w_conv_out, g_mem_norm, w_mem_kv, g_q_mem, g_k_mem, w_o_mem, w_out,
           g_ffn_norm, w_router, b_router, w_e_gate, w_e_up, w_e_down, w_s_gate, w_s_up, w_s_down):
    depth = w_in.shape[0]
    params = (g_attn_norm, w_in, g_q_lora, w_uq, g_q_mla, g_kv_lora, w_uk, w_uv, g_k_mla, w_o_mla, w_dw, b_dw,
              g_conv_ln, b_conv_ln, w_conv_out, g_mem_norm, w_mem_kv, g_q_mem, g_k_mem, w_o_mem, w_out,
              g_ffn_norm, w_router, b_router, w_e_gate, w_e_up, w_e_down, w_s_gate, w_s_up, w_s_down)
    xp, xs = x_prompt, x_sample
    per_layer = []
    for layer in range(depth):
        w = _prep_weights(*[p[layer] for p in params])
        outs = _layer(xp, xs, mem_prompt, cache_latent[layer], cache_krope[layer], state_conv[layer],
                      cache_mem_k[layer], cache_mem_v[layer], page_table, w)
        xp, xs = outs[0], outs[1]
        per_layer.append(outs[2:])
    stacked = tuple(jnp.stack([pl_[i] for pl_ in per_layer]) for i in range(8))
    return (xp, xs) + stacked
```
